```python
import math
import jax, jax.numpy as jnp
from jax import lax
import numpy as np

D_MODEL = 1024
BATCH = 2
SEQ = 8192
DEPTH = 4

N_MIXERS = 2
MIX_WIDTH = 2 * D_MODEL
EPS = 1e-6

MEM_LEN = 256
MEM_HEADS = 4
MEM_WIDTH = MIX_WIDTH // 4
MEM_HEAD_DIM = MEM_WIDTH // MEM_HEADS

SSD_HEAD_DIM = 64
SSD_INNER = MIX_WIDTH - MEM_WIDTH
SSD_HEADS = SSD_INNER // SSD_HEAD_DIM
SSD_GROUPS = 4
SSD_HEADS_PER_GROUP = SSD_HEADS // SSD_GROUPS
SSD_STATE = 128
SSD_CONV = 4
SSD_CHUNK = 128
SSD_CONV_DIM = SSD_INNER + 2 * SSD_GROUPS * SSD_STATE
SSD_IN = SSD_INNER + SSD_CONV_DIM + SSD_HEADS + MEM_WIDTH

SB_HEAD_DIM = 128
SB_WIDTH = MIX_WIDTH - MEM_WIDTH
SB_HEADS = SB_WIDTH // SB_HEAD_DIM
SB_BLOCK = 128
SB_IN = 3 * SB_WIDTH + MEM_WIDTH

FFN_HIDDEN = ((-(-8 * D_MODEL // 3) + 255) // 256) * 256

N_SSD_LAYERS = (DEPTH + 1) // 2
N_SB_LAYERS = DEPTH // 2

kernel_name = "hybrid_ssd_stickbreak_memxattn_trunk"


def _rmsnorm(x, g):
    xf = x.astype(jnp.float32)
    y = xf * lax.rsqrt(jnp.mean(xf * xf, axis=-1, keepdims=True) + EPS)
    return (y * g.astype(jnp.float32)).astype(x.dtype)


def _swiglu(h, w_gate_up, w_down):
    g, u = jnp.split(h @ w_gate_up, 2, axis=-1)
    return (jax.nn.silu(g) * u) @ w_down


def _memory_attention(q, mem_n, w_kv, q_g, k_g):
    b, s = q.shape[:2]
    m = mem_n.shape[1]
    q = _rmsnorm(q.reshape(b, s, MEM_HEADS, MEM_HEAD_DIM), q_g)
    k, v = jnp.split(mem_n @ w_kv, 2, axis=-1)
    k = _rmsnorm(k.reshape(b, m, MEM_HEADS, MEM_HEAD_DIM), k_g)
    v = v.reshape(b, m, MEM_HEADS, MEM_HEAD_DIM)
    scores = jnp.einsum("bshd,bmhd->bhsm", q, k).astype(jnp.float32) * (MEM_HEAD_DIM ** -0.5)
    p = jax.nn.softmax(scores, axis=-1).astype(v.dtype)
    o = jnp.einsum("bhsm,bmhd->bshd", p, v)
    return o.reshape(b, s, MEM_WIDTH)


def _causal_depthwise_conv(u, w, bias):
    c = u.shape[-1]
    out = lax.conv_general_dilated(
        u, w[:, None, :].astype(u.dtype), window_strides=(1,),
        padding=[(SSD_CONV - 1, 0)], dimension_numbers=("NWC", "WIO", "NWC"),
        feature_group_count=c)
    return out + bias


def _ssd_chunked(xs, dt, a, bm, cm):
    b, s = xs.shape[:2]
    nc = s // SSD_CHUNK
    L, G, R, P, N = SSD_CHUNK, SSD_GROUPS, SSD_HEADS_PER_GROUP, SSD_HEAD_DIM, SSD_STATE
    f32 = jnp.float32
    xdt = (xs.astype(f32) * dt[..., None]).reshape(b, nc, L, G, R, P)
    da = (dt * a).reshape(b, nc, L, G, R)
    bm = bm.astype(f32).reshape(b, nc, L, G, N)
    cm = cm.astype(f32).reshape(b, nc, L, G, N)
    cs = jnp.cumsum(da, axis=2)
    causal = jnp.tril(jnp.ones((L, L), dtype=bool))[:, :, None, None]
    seg = jnp.where(causal, cs[:, :, :, None] - cs[:, :, None, :], -jnp.inf)
    w_intra = jnp.einsum("bclgn,bcsgn->bclsg", cm, bm)[..., None] * jnp.exp(seg)
    y_diag = jnp.einsum("bclsgr,bcsgrp->bclgrp", w_intra, xdt)
    decay_to_end = jnp.exp(cs[:, :, -1:] - cs)
    chunk_states = jnp.einsum("bclgn,bclgr,bclgrp->bcgrpn", bm, decay_to_end, xdt)
    chunk_decay = jnp.exp(cs[:, :, -1])

    def step(state, inp):
        st, dec = inp
        return state * dec[..., None, None] + st, state

    h0 = jnp.zeros((b, G, R, P, N), f32)
    _, h_in = lax.scan(step, h0, (jnp.moveaxis(chunk_states, 1, 0), jnp.moveaxis(chunk_decay, 1, 0)))
    h_in = jnp.moveaxis(h_in, 0, 1)
    y_off = jnp.einsum("bclgn,bcgrpn->bclgrp", cm, h_in) * jnp.exp(cs)[..., None]
    return (y_diag + y_off).reshape(b, s, SSD_HEADS, P)


def _ssd_mix(h, mem_n, w_in, conv_w, conv_b, dt_bias, a_log, d_skip, norm_g, w_out,
             mem_w_kv, mq_g, mk_g):
    b, s, _ = h.shape
    proj = h @ w_in
    z, xbc, dt_raw, q_mem = jnp.split(
        proj, [SSD_INNER, SSD_INNER + SSD_CONV_DIM, SSD_INNER + SSD_CONV_DIM + SSD_HEADS], axis=-1)
    xbc = jax.nn.silu(_causal_depthwise_conv(xbc, conv_w, conv_b))
    xs, bm, cm = jnp.split(xbc, [SSD_INNER, SSD_INNER + SSD_GROUPS * SSD_STATE], axis=-1)
    dt = jax.nn.softplus(dt_raw.astype(jnp.float32) + dt_bias.astype(jnp.float32))
    a = -jnp.exp(a_log.astype(jnp.float32))
    xs = xs.reshape(b, s, SSD_HEADS, SSD_HEAD_DIM)
    y = _ssd_chunked(xs, dt, a,
                     bm.reshape(b, s, SSD_GROUPS, SSD_STATE),
                     cm.reshape(b, s, SSD_GROUPS, SSD_STATE))
    y = y + d_skip.astype(jnp.float32)[:, None] * xs.astype(jnp.float32)
    y = y.reshape(b, s, SSD_INNER) * jax.nn.silu(z.astype(jnp.float32))
    gs = SSD_INNER // SSD_GROUPS
    y = _rmsnorm(y.reshape(b, s, SSD_GROUPS, gs), norm_g.reshape(SSD_GROUPS, gs))
    y = y.reshape(b, s, SSD_INNER).astype(h.dtype)
    o_mem = _memory_attention(q_mem, mem_n, mem_w_kv, mq_g, mk_g)
    return jnp.concatenate([y, o_mem], axis=-1) @ w_out


def _stick_breaking(q, k, v):
    seq = q.shape[1]
    scale = SB_HEAD_DIM ** -0.5
    outs = []
    for blk in range(seq // SB_BLOCK):
        t0 = blk * SB_BLOCK
        end = t0 + SB_BLOCK
        qb, kb, vb = q[:, t0:end], k[:, :end], v[:, :end]
        z = jnp.einsum("bthd,bshd->bhts", qb, kb).astype(jnp.float32) * scale
        causal = jnp.arange(end)[None, :] < jnp.arange(t0, end)[:, None]
        log_1m = jnp.where(causal, jax.nn.log_sigmoid(-z), 0.0)
        log_survive = lax.cumsum(log_1m, axis=3, reverse=True) - log_1m
        w = jnp.where(causal, jnp.exp(jax.nn.log_sigmoid(z) + log_survive), 0.0)
        outs.append(jnp.einsum("bhts,bshd->bthd", w.astype(vb.dtype), vb))
    return jnp.concatenate(outs, axis=1)


def _sb_mix(h, mem_n, w_in, q_g, k_g, w_out, mem_w_kv, mq_g, mk_g):
    b, s, _ = h.shape
    q, k, v, q_mem = jnp.split(h @ w_in, [SB_WIDTH, 2 * SB_WIDTH, 3 * SB_WIDTH], axis=-1)
    shp = (b, s, SB_HEADS, SB_HEAD_DIM)
    q = _rmsnorm(q.reshape(shp), q_g)
    k = _rmsnorm(k.reshape(shp), k_g)
    o = _stick_breaking(q, k, v.reshape(shp)).reshape(b, s, SB_WIDTH)
    o_mem = _memory_attention(q_mem, mem_n, mem_w_kv, mq_g, mk_g)
    return jnp.concatenate([o, o_mem], axis=-1) @ w_out


def setup_inputs(seed: int = 0) -> dict:
    key = jax.random.key(seed)
    ks = jax.random.split(key, 24)
    nrm = jax.random.normal
    res_scale = 1.0 / math.sqrt(2 * DEPTH)
    dt = jnp.exp(jax.random.uniform(ks[10], (N_SSD_LAYERS, SSD_HEADS))
                 * (math.log(0.1) - math.log(0.001)) + math.log(0.001))
    return {
        "x": nrm(ks[0], (BATCH, SEQ, D_MODEL), jnp.float32),
        "mem": nrm(ks[1], (BATCH, MEM_LEN, D_MODEL), jnp.float32),
        "mix_norm_g": 1.0 + 0.02 * nrm(ks[2], (DEPTH, D_MODEL)),
        "ffn_norm_g": 1.0 + 0.02 * nrm(ks[3], (DEPTH, D_MODEL)),
        "mem_norm_g": 1.0 + 0.02 * nrm(ks[4], (D_MODEL,)),
        "mem_w_kv": nrm(ks[5], (DEPTH, D_MODEL, 2 * MEM_WIDTH)) * D_MODEL ** -0.5,
        "mem_q_norm_g": 1.0 + 0.02 * nrm(ks[6], (DEPTH, MEM_HEAD_DIM)),
        "mem_k_norm_g": 1.0 + 0.02 * nrm(ks[7], (DEPTH, MEM_HEAD_DIM)),
        "ssd_w_in": nrm(ks[8], (N_SSD_LAYERS, D_MODEL, SSD_IN)) * D_MODEL ** -0.5,
        "ssd_conv_w": nrm(ks[9], (N_SSD_LAYERS, SSD_CONV, SSD_CONV_DIM)) * SSD_CONV ** -0.5,
        "ssd_conv_b": 0.02 * nrm(ks[11], (N_SSD_LAYERS, SSD_CONV_DIM)),
        "ssd_dt_bias": dt + jnp.log(-jnp.expm1(-dt)),
        "ssd_a_log": jnp.log(jax.random.uniform(ks[12], (N_SSD_LAYERS, SSD_HEADS), minval=1.0, maxval=16.0)),
        "ssd_d": 1.0 + 0.02 * nrm(ks[13], (N_SSD_LAYERS, SSD_HEADS)),
        "ssd_norm_g": 1.0 + 0.02 * nrm(ks[14], (N_SSD_LAYERS, SSD_INNER)),
        "ssd_w_out": nrm(ks[15], (N_SSD_LAYERS, MIX_WIDTH, D_MODEL)) * MIX_WIDTH ** -0.5 * res_scale,
        "sb_w_in": nrm(ks[16], (N_SB_LAYERS, D_MODEL, SB_IN)) * D_MODEL ** -0.5,
        "sb_q_norm_g": 1.0 + 0.02 * nrm(ks[17], (N_SB_LAYERS, SB_HEAD_DIM)),
        "sb_k_norm_g": 1.0 + 0.02 * nrm(ks[18], (N_SB_LAYERS, SB_HEAD_DIM)),
        "sb_w_out": nrm(ks[19], (N_SB_LAYERS, MIX_WIDTH, D_MODEL)) * MIX_WIDTH ** -0.5 * res_scale,
        "ffn_w_gate_up": nrm(ks[20], (DEPTH, D_MODEL, 2 * FFN_HIDDEN)) * D_MODEL ** -0.5,
        "ffn_w_down": nrm(ks[21], (DEPTH, FFN_HIDDEN, D_MODEL)) * FFN_HIDDEN ** -0.5 * res_scale,
    }


def reference(x, mem, mix_norm_g, ffn_norm_g, mem_norm_g, mem_w_kv, mem_q_norm_g, mem_k_norm_g,
              ssd_w_in, ssd_conv_w, ssd_conv_b, ssd_dt_bias, ssd_a_log, ssd_d, ssd_norm_g, ssd_w_out,
              sb_w_in, sb_q_norm_g, sb_k_norm_g, sb_w_out, ffn_w_gate_up, ffn_w_down):
    mem_n = _rmsnorm(mem, mem_norm_g)
    for i in range(DEPTH):
        j = i // N_MIXERS
        h = _rmsnorm(x, mix_norm_g[i])
        if i % N_MIXERS == 0:
            mix = _ssd_mix(h, mem_n, ssd_w_in[j], ssd_conv_w[j], ssd_conv_b[j], ssd_dt_bias[j],
                           ssd_a_log[j], ssd_d[j], ssd_norm_g[j], ssd_w_out[j],
                           mem_w_kv[i], mem_q_norm_g[i], mem_k_norm_g[i])
        else:
            mix = _sb_mix(h, mem_n, sb_w_in[j], sb_q_norm_g[j], sb_k_norm_g[j], sb_w_out[j],
                          mem_w_kv[i], mem_q_norm_g[i], mem_k_norm_g[i])
        x = x + mix
        x = x + _swiglu(_rmsnorm(x, ffn_norm_g[i]), ffn_w_gate_up[i], ffn_w_down[i])
    return x
```

```python
import functools
import math

import jax
import jax.numpy as jnp
from jax import lax
from jax.experimental import pallas as pl
from jax.experimental.pallas import tpu as pltpu

D_MODEL = 1024
DEPTH = 4
MIX_WIDTH = 2 * D_MODEL
EPS = 1e-6
MEM_HEADS = 4
MEM_WIDTH = MIX_WIDTH // 4
MEM_HEAD_DIM = MEM_WIDTH // MEM_HEADS
SSD_HEAD_DIM = 64
SSD_INNER = MIX_WIDTH - MEM_WIDTH
SSD_HEADS = SSD_INNER // SSD_HEAD_DIM
SSD_GROUPS = 4
SSD_HPG = SSD_HEADS // SSD_GROUPS
SSD_STATE = 128
SSD_CONV = 4
SSD_CHUNK = 128
SSD_BC = SSD_GROUPS * SSD_STATE
SSD_CONV_DIM = SSD_INNER + 2 * SSD_BC
SB_HEAD_DIM = 128
SB_WIDTH = MIX_WIDTH - MEM_WIDTH
SB_HEADS = SB_WIDTH // SB_HEAD_DIM
FFN_HIDDEN = 2816

LANE = 128
SUBLANE = 8
VMEM_LIMIT_BYTES = 56 * 1024 * 1024

LOG2E = 1.4426950408889634
F32 = jnp.float32
BF16 = jnp.bfloat16
HIGHEST = lax.Precision.HIGHEST

TOKEN_TILE = 512
SB_Q_TILE = 512
SB_KEY_BLOCK = 256
FFN_HIDDEN_CHUNK = 256


def _params(n_axes):
    return pltpu.CompilerParams(
        dimension_semantics=("arbitrary",) * n_axes,
        vmem_limit_bytes=VMEM_LIMIT_BYTES,
    )


def _rms_rows(xf, gain):
    ms = jnp.mean(xf * xf, axis=-1, keepdims=True)
    return xf * lax.rsqrt(ms + EPS) * gain


def _resident(shape):
    nd = len(shape)
    return pl.BlockSpec(shape, lambda *_: (0,) * nd, pipeline_mode=pl.Buffered(1))


def _memkv_kernel(mem_ref, g_ref, w_ref, kg_ref, kt_ref, v_ref, *, batch, mlen):
    mn = _rms_rows(mem_ref[...], g_ref[...]).astype(BF16)
    kv = jnp.dot(mn, w_ref[0], preferred_element_type=F32)
    kg = kg_ref[0]
    for b in range(batch):
        for h in range(MEM_HEADS):
            k = kv[b * mlen:(b + 1) * mlen, h * LANE:(h + 1) * LANE]
            k = _rms_rows(k, kg)
            kt_ref[0, b, h] = k.T.astype(BF16)
            c0 = MEM_WIDTH + h * LANE
            v_ref[0, b, h] = kv[b * mlen:(b + 1) * mlen, c0:c0 + LANE].astype(BF16)


def _mem_kv(mem, mem_norm_g, w_kv_bf16, k_norm_g):
    batch, mlen, d = mem.shape
    depth = w_kv_bf16.shape[0]
    kern = functools.partial(_memkv_kernel, batch=batch, mlen=mlen)
    return pl.pallas_call(
        kern,
        grid=(depth,),
        in_specs=[
            pl.BlockSpec((batch * mlen, d), lambda i: (0, 0)),
            pl.BlockSpec((1, d), lambda i: (0, 0)),
            pl.BlockSpec((1, d, 2 * MEM_WIDTH), lambda i: (i, 0, 0)),
            pl.BlockSpec((1, 1, MEM_HEAD_DIM), lambda i: (i, 0, 0)),
        ],
        out_specs=[
            pl.BlockSpec((1, batch, MEM_HEADS, MEM_HEAD_DIM, mlen), lambda i: (i, 0, 0, 0, 0)),
            pl.BlockSpec((1, batch, MEM_HEADS, mlen, MEM_HEAD_DIM), lambda i: (i, 0, 0, 0, 0)),
        ],
        out_shape=[
            jax.ShapeDtypeStruct((depth, batch, MEM_HEADS, MEM_HEAD_DIM, mlen), BF16),
            jax.ShapeDtypeStruct((depth, batch, MEM_HEADS, mlen, MEM_HEAD_DIM), BF16),
        ],
        compiler_params=_params(1),
        name="mem_kv",
    )(mem.reshape(batch * mlen, d), mem_norm_g.reshape(1, d), w_kv_bf16,
      k_norm_g.reshape(depth, 1, MEM_HEAD_DIM))


def _in_proj_kernel(x_ref, g_ref, w_ref, hg_ref, *out_refs, segs, norm_groups, chunk):
    xn = _rms_rows(x_ref[...], g_ref[...]).astype(BF16)
    for (c0, width), o_ref in zip(segs, out_refs):
        for cc in range(0, width, chunk):
            cw = min(chunk, width - cc)
            acc = jnp.dot(xn, w_ref[:, c0 + cc:c0 + cc + cw], preferred_element_type=F32)
            for gi in range(cw // LANE):
                col = c0 + cc + gi * LANE
                y = acc[:, gi * LANE:(gi + 1) * LANE]
                if col // LANE in norm_groups:
                    y = _rms_rows(y, hg_ref[:, col:col + LANE])
                o_ref[:, cc + gi * LANE:cc + (gi + 1) * LANE] = y.astype(o_ref.dtype)


def _in_proj(x2d, norm_g, w_bf16, head_gain, segs, out_dtypes, norm_groups, name):
    t, d = x2d.shape
    n = w_bf16.shape[1]
    tm = min(TOKEN_TILE, t)
    kern = functools.partial(_in_proj_kernel, segs=segs, norm_groups=frozenset(norm_groups),
                             chunk=4 * LANE)
    return pl.pallas_call(
        kern,
        grid=(t // tm,),
        in_specs=[
            pl.BlockSpec((tm, d), lambda i: (i, 0)),
            _resident((1, d)),
            _resident((d, n)),
            _resident((1, n)),
        ],
        out_specs=[pl.BlockSpec((tm, w), lambda i: (i, 0)) for (_, w) in segs],
        out_shape=[jax.ShapeDtypeStruct((t, w), dt) for (_, w), dt in zip(segs, out_dtypes)],
        compiler_params=_params(1),
        name=name,
    )(x2d, norm_g.reshape(1, d), w_bf16, head_gain)


def _ssd_kernel(z_ref, xbc_ref, dt_ref, cw_ref, cb_ref, dtb_ref, alog_ref, dexp_ref, ng_ref,
                o_ref, cbuf, state):
    L = SSD_CHUNK
    gw = SSD_HPG * SSD_HEAD_DIM
    c = pl.program_id(1)

    @pl.when(c == 0)
    def _():
        cbuf[0:SUBLANE, :] = jnp.zeros((SUBLANE, SSD_CONV_DIM), F32)
        state[...] = jnp.zeros(state.shape, F32)

    cbuf[SUBLANE:SUBLANE + L, :] = xbc_ref[...].astype(F32)
    acc = jnp.broadcast_to(cb_ref[...], (L, SSD_CONV_DIM))
    for k in range(SSD_CONV):
        acc = acc + cw_ref[k:k + 1, :] * cbuf[pl.ds(SUBLANE - (SSD_CONV - 1) + k, L), :]
    cbuf[0:SUBLANE, :] = cbuf[L:L + SUBLANE, :]
    u = acc * jax.nn.sigmoid(acc)
    xs = u[:, :SSD_INNER]
    bm = u[:, SSD_INNER:SSD_INNER + SSD_BC]
    cm = u[:, SSD_INNER + SSD_BC:]

    lane = lax.broadcasted_iota(jnp.int32, (1, LANE), 1)
    traw = dt_ref[...] + dtb_ref[...]
    dt = jnp.maximum(traw, 0.0) + jnp.log1p(jnp.exp(-jnp.abs(traw)))
    a = jnp.where(lane < SSD_HEADS, -jnp.exp(alog_ref[...]), 0.0)
    da = dt * a
    ri = lax.broadcasted_iota(jnp.int32, (L, L), 0)
    ci = lax.broadcasted_iota(jnp.int32, (L, L), 1)
    causal = ri >= ci
    tri = causal.astype(F32)
    cs = jnp.dot(tri, da, precision=HIGHEST, preferred_element_type=F32)
    cs_t = cs.T
    cs_last = cs[L - 1:L, :]
    ecs = jnp.exp(cs)
    dte = jnp.exp(cs_last - cs)
    erow = lax.broadcasted_iota(jnp.int32, (LANE, SSD_INNER), 0)
    ecol = lax.broadcasted_iota(jnp.int32, (LANE, SSD_INNER), 1) // SSD_HEAD_DIM
    expand = (erow == ecol).astype(F32)
    dt_x = jnp.dot(dt, expand, precision=HIGHEST, preferred_element_type=F32)
    ecs_x = jnp.dot(ecs, expand, precision=HIGHEST, preferred_element_type=F32)
    dte_x = jnp.dot(dte, expand, precision=HIGHEST, preferred_element_type=F32)
    xdt = xs * dt_x
    xdt_b = xdt.astype(BF16)
    xdte_b = (xdt * dte_x).astype(BF16)
    cdec_x = ecs_x[L - 1:L, :]
    lo_half = lax.broadcasted_iota(jnp.int32, (L, LANE), 1) < SSD_HEAD_DIM

    z = z_ref[...].astype(F32)
    gate = z * jax.nn.sigmoid(z)
    for g in range(SSD_GROUPS):
        g0 = g * gw
        cg = cm[:, g * SSD_STATE:(g + 1) * SSD_STATE].astype(BF16)
        bg_t = bm[:, g * SSD_STATE:(g + 1) * SSD_STATE].T.astype(BF16)
        cbm = jnp.dot(cg, bg_t, preferred_element_type=F32)
        st = state[g]
        y_g = jnp.dot(cg, st.astype(BF16), preferred_element_type=F32) * ecs_x[:, g0:g0 + gw]
        st_c = jnp.dot(bg_t, xdte_b[:, g0:g0 + gw], preferred_element_type=F32)
        state[g] = st * cdec_x[:, g0:g0 + gw] + st_c
        pairs = []
        for pr in range(SSD_HPG // 2):
            p0 = g0 + pr * LANE
            xp = xdt_b[:, p0:p0 + LANE]
            yd = None
            for half in range(2):
                h = g * SSD_HPG + pr * 2 + half
                seg = jnp.broadcast_to(cs[:, h:h + 1], (L, L)) - cs_t[h:h + 1, :]
                dec = jnp.where(causal, jnp.exp(seg), 0.0)
                wm = (cbm * dec).astype(BF16)
                keep = lo_half if half == 0 else jnp.logical_not(lo_half)
                part = jnp.dot(wm, jnp.where(keep, xp, jnp.zeros_like(xp)),
                               preferred_element_type=F32)
                yd = part if yd is None else yd + part
            pairs.append(yd)
        y_g = y_g + jnp.concatenate(pairs, axis=1) + dexp_ref[:, g0:g0 + gw] * xs[:, g0:g0 + gw]
        y_g = y_g * gate[:, g0:g0 + gw]
        o_ref[:, g0:g0 + gw] = _rms_rows(y_g, ng_ref[:, g0:g0 + gw]).astype(o_ref.dtype)


def _ssd(z, xbc, dt, conv_w, conv_b, dt_bias, a_log, d_skip, norm_g, batch, seq):
    t = batch * seq
    nc = seq // SSD_CHUNK
    L = SSD_CHUNK
    pad = LANE - SSD_HEADS
    row = lambda b, c: (b * nc + c, 0)
    return pl.pallas_call(
        _ssd_kernel,
        grid=(batch, nc),
        in_specs=[
            pl.BlockSpec((L, SSD_INNER), row),
            pl.BlockSpec((L, SSD_CONV_DIM), row),
            pl.BlockSpec((L, LANE), row),
            _resident((SSD_CONV, SSD_CONV_DIM)),
            _resident((1, SSD_CONV_DIM)),
            _resident((1, LANE)),
            _resident((1, LANE)),
            _resident((1, SSD_INNER)),
            _resident((1, SSD_INNER)),
        ],
        out_specs=pl.BlockSpec((L, SSD_INNER), row),
        out_shape=jax.ShapeDtypeStruct((t, SSD_INNER), BF16),
        scratch_shapes=[
            pltpu.VMEM((L + 2 * SUBLANE, SSD_CONV_DIM), F32),
            pltpu.VMEM((SSD_GROUPS, SSD_STATE, SSD_HPG * SSD_HEAD_DIM), F32),
        ],
        compiler_params=_params(2),
        name="ssd_mix",
    )(z, xbc, dt, conv_w, conv_b.reshape(1, -1),
      jnp.pad(dt_bias, (0, pad)).reshape(1, LANE), jnp.pad(a_log, (0, pad)).reshape(1, LANE),
      jnp.repeat(d_skip, SSD_HEAD_DIM).reshape(1, SSD_INNER), norm_g.reshape(1, SSD_INNER))


def _sb_kernel(q_ref, k_ref, v_ref, o_ref, acc_ref, carry_ref, *, tq, kb):
    qi = pl.program_id(2)
    q = q_ref[...]
    acc_ref[...] = jnp.zeros(acc_ref.shape, F32)
    carry_ref[...] = jnp.zeros(carry_ref.shape, F32)
    tj = lax.broadcasted_iota(jnp.int32, (kb, kb), 0)
    ts = lax.broadcasted_iota(jnp.int32, (kb, kb), 1)
    suffix = (tj > ts).astype(BF16)
    rows = lax.broadcasted_iota(jnp.int32, (tq, kb), 0)
    cols = lax.broadcasted_iota(jnp.int32, (tq, kb), 1)
    nd = tq // kb

    def process(kstart, mask):
        kblk = k_ref[pl.ds(kstart, kb), :]
        vblk = v_ref[pl.ds(kstart, kb), :]
        z2 = lax.dot_general(q, kblk, (((1,), (1,)), ((), ())), preferred_element_type=F32)
        soft = jnp.log(1.0 + jnp.exp2(-jnp.abs(z2))) * LOG2E
        l1m = -(jnp.maximum(z2, 0.0) + soft)
        if mask is not None:
            l1m = jnp.where(mask, l1m, 0.0)
        lloc = jnp.dot(l1m.astype(BF16), suffix, preferred_element_type=F32)
        carry = carry_ref[...]
        e = z2 + l1m + lloc + jnp.concatenate([carry] * (kb // LANE), axis=1)
        w = jnp.exp2(e)
        if mask is not None:
            w = jnp.where(mask, w, 0.0)
        acc_ref[...] += jnp.dot(w.astype(BF16), vblk, preferred_element_type=F32)
        carry_ref[...] = carry + (lloc[:, 0:1] + l1m[:, 0:1])

    for d in reversed(range(nd)):
        process(pl.multiple_of(qi * tq + d * kb, kb), (cols + d * kb) < rows)

    nbulk = qi * nd

    def body(i, c):
        process(pl.multiple_of((nbulk - 1 - i) * kb, kb), None)
        return c

    lax.fori_loop(0, nbulk, body, 0)
    o_ref[...] = acc_ref[...].astype(o_ref.dtype)


def _sb_attn(qkv, batch, seq):
    t = batch * seq
    tq = min(SB_Q_TILE, seq)
    kb = min(SB_KEY_BLOCK, tq)
    nq = seq // tq
    dh = SB_HEAD_DIM
    kern = functools.partial(_sb_kernel, tq=tq, kb=kb)
    return pl.pallas_call(
        kern,
        grid=(batch, SB_HEADS, nq),
        in_specs=[
            pl.BlockSpec((tq, dh), lambda b, h, i: (b * nq + i, h)),
            pl.BlockSpec((seq, dh), lambda b, h, i: (b, SB_HEADS + h)),
            pl.BlockSpec((seq, dh), lambda b, h, i: (b, 2 * SB_HEADS + h)),
        ],
        out_specs=pl.BlockSpec((tq, dh), lambda b, h, i: (b * nq + i, h)),
        out_shape=jax.ShapeDtypeStruct((t, SB_WIDTH), BF16),
        scratch_shapes=[pltpu.VMEM((tq, dh), F32), pltpu.VMEM((tq, LANE), F32)],
        compiler_params=_params(3),
        name="sb_attn",
    )(qkv, qkv, qkv)


def _out_kernel(x_ref, y_ref, qm_ref, kt_ref, v_ref, w_ref, o_ref):
    qm = qm_ref[...]
    heads = []
    for h in range(MEM_HEADS):
        s = jnp.dot(qm[:, h * LANE:(h + 1) * LANE], kt_ref[0, h], preferred_element_type=F32)
        m = jnp.max(s, axis=-1, keepdims=True)
        p = jnp.exp2(s - m)
        l = jnp.sum(p, axis=-1, keepdims=True)
        o = jnp.dot(p.astype(BF16), v_ref[0, h], preferred_element_type=F32) / l
        heads.append(o.astype(BF16))
    om = jnp.concatenate(heads, axis=1)
    ny = y_ref.shape[1]
    acc = x_ref[...] + jnp.dot(y_ref[...], w_ref[:ny, :], preferred_element_type=F32)
    o_ref[...] = acc + jnp.dot(om, w_ref[ny:, :], preferred_element_type=F32)


def _out_proj(x2d, y, qm_arr, qm_col_block, mem_kt, mem_v, w_out_bf16, seq):
    t, d = x2d.shape
    tm = min(TOKEN_TILE, seq)
    per_batch = seq // tm
    ny = y.shape[1]
    mlen = mem_v.shape[2]
    return pl.pallas_call(
        _out_kernel,
        grid=(t // tm,),
        in_specs=[
            pl.BlockSpec((tm, d), lambda i: (i, 0)),
            pl.BlockSpec((tm, ny), lambda i: (i, 0)),
            pl.BlockSpec((tm, MEM_WIDTH), lambda i: (i, qm_col_block)),
            pl.BlockSpec((1, MEM_HEADS, MEM_HEAD_DIM, mlen), lambda i: (i // per_batch, 0, 0, 0)),
            pl.BlockSpec((1, MEM_HEADS, mlen, MEM_HEAD_DIM), lambda i: (i // per_batch, 0, 0, 0)),
            _resident((MIX_WIDTH, d)),
        ],
        out_specs=pl.BlockSpec((tm, d), lambda i: (i, 0)),
        out_shape=jax.ShapeDtypeStruct((t, d), F32),
        compiler_params=_params(1),
        name="out_proj",
    )(x2d, y, qm_arr, mem_kt, mem_v, w_out_bf16)


def _ffn_kernel(x_ref, g_ref, wgu_ref, wd_ref, o_ref, *, hidden, chunk):
    x = x_ref[...]
    xn = _rms_rows(x, g_ref[...]).astype(BF16)
    acc = x
    for c0 in range(0, hidden, chunk):
        gt = jnp.dot(xn, wgu_ref[:, c0:c0 + chunk], preferred_element_type=F32)
        up = jnp.dot(xn, wgu_ref[:, hidden + c0:hidden + c0 + chunk], preferred_element_type=F32)
        hh = (gt * jax.nn.sigmoid(gt) * up).astype(BF16)
        acc = acc + jnp.dot(hh, wd_ref[c0:c0 + chunk, :], preferred_element_type=F32)
    o_ref[...] = acc


def _ffn(x2d, norm_g, wgu_bf16, wd_bf16):
    t, d = x2d.shape
    hidden = wd_bf16.shape[0]
    tm = min(TOKEN_TILE, t)
    kern = functools.partial(_ffn_kernel, hidden=hidden, chunk=FFN_HIDDEN_CHUNK)
    return pl.pallas_call(
        kern,
        grid=(t // tm,),
        in_specs=[
            pl.BlockSpec((tm, d), lambda i: (i, 0)),
            _resident((1, d)),
            _resident((d, 2 * hidden)),
            _resident((hidden, d)),
        ],
        out_specs=pl.BlockSpec((tm, d), lambda i: (i, 0)),
        out_shape=jax.ShapeDtypeStruct((t, d), F32),
        compiler_params=_params(1),
        name="ffn",
    )(x2d, norm_g.reshape(1, d), wgu_bf16, wd_bf16)


def _ssd_layer(x2d, mix_g, w_in, conv_w, conv_b, dt_bias, a_log, d_skip, norm_g, w_out,
               mq_g, mem_kt, mem_v, batch, seq):
    xbc_end = SSD_INNER + SSD_CONV_DIM
    dt_end = xbc_end + SSD_HEADS
    w = jnp.concatenate(
        [w_in[:, :xbc_end], w_in[:, dt_end:], w_in[:, xbc_end:dt_end],
         jnp.zeros((w_in.shape[0], LANE - SSD_HEADS), w_in.dtype)], axis=1).astype(BF16)
    qm0 = xbc_end
    segs = ((0, SSD_INNER), (SSD_INNER, SSD_CONV_DIM), (qm0, MEM_WIDTH), (qm0 + MEM_WIDTH, LANE))
    gain = jnp.ones((w.shape[1],), F32).at[qm0:qm0 + MEM_WIDTH].set(
        jnp.tile(mq_g * (MEM_HEAD_DIM ** -0.5 * LOG2E), MEM_HEADS)).reshape(1, -1)
    groups = range(qm0 // LANE, (qm0 + MEM_WIDTH) // LANE)
    z, xbc, qm, dt = _in_proj(x2d, mix_g, w, gain, segs, (BF16, BF16, BF16, F32), groups,
                              "in_proj_ssd")
    y = _ssd(z, xbc, dt, conv_w, conv_b, dt_bias, a_log, d_skip, norm_g, batch, seq)
    return _out_proj(x2d, y, qm, 0, mem_kt, mem_v, w_out.astype(BF16), seq)


def _sb_layer(x2d, mix_g, w_in, q_g, k_g, w_out, mq_g, mem_kt, mem_v, batch, seq):
    n = w_in.shape[1]
    qm0 = 3 * SB_WIDTH
    gain = jnp.concatenate([
        jnp.tile(q_g * (SB_HEAD_DIM ** -0.5 * LOG2E), SB_HEADS),
        jnp.tile(k_g, SB_HEADS),
        jnp.ones((SB_WIDTH,), F32),
        jnp.tile(mq_g * (MEM_HEAD_DIM ** -0.5 * LOG2E), MEM_HEADS)]).reshape(1, n)
    groups = list(range(0, 2 * SB_HEADS)) + list(range(qm0 // LANE, n // LANE))
    (qkv,) = _in_proj(x2d, mix_g, w_in.astype(BF16), gain, ((0, n),), (BF16,), groups,
                      "in_proj_sb")
    o = _sb_attn(qkv, batch, seq)
    return _out_proj(x2d, o, qkv, qm0 // MEM_WIDTH, mem_kt, mem_v, w_out.astype(BF16), seq)


def kernel(x, mem, mix_norm_g, ffn_norm_g, mem_norm_g, mem_w_kv, mem_q_norm_g, mem_k_norm_g, ssd_w_in, ssd_conv_w, ssd_conv_b, ssd_dt_bias, ssd_a_log, ssd_d, ssd_norm_g, ssd_w_out, sb_w_in, sb_q_norm_g, sb_k_norm_g, sb_w_out, ffn_w_gate_up, ffn_w_down):
    batch, seq, d = x.shape
    x2d = x.reshape(batch * seq, d)
    mem_kt, mem_v = _mem_kv(mem, mem_norm_g, mem_w_kv.astype(BF16), mem_k_norm_g)
    for i in range(DEPTH):
        j = i // 2
        if i % 2 == 0:
            x2d = _ssd_layer(x2d, mix_norm_g[i], ssd_w_in[j], ssd_conv_w[j], ssd_conv_b[j],
                             ssd_dt_bias[j], ssd_a_log[j], ssd_d[j], ssd_norm_g[j], ssd_w_out[j],
                             mem_q_norm_g[i], mem_kt[i], mem_v[i], batch, seq)
        else:
            x2d = _sb_layer(x2d, mix_norm_g[i], sb_w_in[j], sb_q_norm_g[j], sb_k_norm_g[j],
                            sb_w_out[j], mem_q_norm_g[i], mem_kt[i], mem_v[i], batch, seq)
        x2d = _ffn(x2d, ffn_norm_g[i], ffn_w_gate_up[i].astype(BF16), ffn_w_down[i].astype(BF16))
    return x2d.reshape(batch, seq, d)
```

```python
import functools
import math

import jax
import jax.numpy as jnp
from jax import lax
from jax.experimental import pallas as pl
from jax.experimental.pallas import tpu as pltpu

D_MODEL = 1024
DEPTH = 4
MIX_WIDTH = 2 * D_MODEL
EPS = 1e-6
MEM_HEADS = 4
MEM_WIDTH = MIX_WIDTH // 4
MEM_HEAD_DIM = MEM_WIDTH // MEM_HEADS
SSD_HEAD_DIM = 64
SSD_INNER = MIX_WIDTH - MEM_WIDTH
SSD_HEADS = SSD_INNER // SSD_HEAD_DIM
SSD_GROUPS = 4
SSD_HPG = SSD_HEADS // SSD_GROUPS
SSD_STATE = 128
SSD_CONV = 4
SSD_CHUNK = 128
SSD_BC = SSD_GROUPS * SSD_STATE
SSD_CONV_DIM = SSD_INNER + 2 * SSD_BC
SB_HEAD_DIM = 128
SB_WIDTH = MIX_WIDTH - MEM_WIDTH
SB_HEADS = SB_WIDTH // SB_HEAD_DIM
FFN_HIDDEN = 2816

LANE = 128
SUBLANE = 8
VMEM_LIMIT_BYTES = 56 * 1024 * 1024

LOG2E = 1.4426950408889634
F32 = jnp.float32
BF16 = jnp.bfloat16
HIGHEST = lax.Precision.HIGHEST

TOKEN_TILE = 512
SB_KEY_BLOCK = 256
SB_SLOTS = 8
SB_ROW_CHUNK = 64
FFN_HIDDEN_CHUNK = 256


def _params(n_axes, flags=None):
    return pltpu.CompilerParams(
        dimension_semantics=("arbitrary",) * n_axes,
        vmem_limit_bytes=VMEM_LIMIT_BYTES,
        flags=flags,
    )


def _rms_rows(xf, gain):
    ms = jnp.mean(xf * xf, axis=-1, keepdims=True)
    return xf * lax.rsqrt(ms + EPS) * gain


def _resident(shape):
    nd = len(shape)
    return pl.BlockSpec(shape, lambda *_: (0,) * nd, pipeline_mode=pl.Buffered(1))


def _memkv_kernel(mem_ref, g_ref, w_ref, kg_ref, kt_ref, v_ref, *, batch, mlen):
    mn = _rms_rows(mem_ref[...], g_ref[...]).astype(BF16)
    kv = jnp.dot(mn, w_ref[0], preferred_element_type=F32)
    kg = kg_ref[0]
    for b in range(batch):
        for h in range(MEM_HEADS):
            k = kv[b * mlen:(b + 1) * mlen, h * LANE:(h + 1) * LANE]
            k = _rms_rows(k, kg)
            kt_ref[0, b, h] = k.T.astype(BF16)
            c0 = MEM_WIDTH + h * LANE
            v_ref[0, b, h] = kv[b * mlen:(b + 1) * mlen, c0:c0 + LANE].astype(BF16)


def _mem_kv(mem, mem_norm_g, w_kv_bf16, k_norm_g):
    batch, mlen, d = mem.shape
    depth = w_kv_bf16.shape[0]
    kern = functools.partial(_memkv_kernel, batch=batch, mlen=mlen)
    return pl.pallas_call(
        kern,
        grid=(depth,),
        in_specs=[
            pl.BlockSpec((batch * mlen, d), lambda i: (0, 0)),
            pl.BlockSpec((1, d), lambda i: (0, 0)),
            pl.BlockSpec((1, d, 2 * MEM_WIDTH), lambda i: (i, 0, 0)),
            pl.BlockSpec((1, 1, MEM_HEAD_DIM), lambda i: (i, 0, 0)),
        ],
        out_specs=[
            pl.BlockSpec((1, batch, MEM_HEADS, MEM_HEAD_DIM, mlen), lambda i: (i, 0, 0, 0, 0)),
            pl.BlockSpec((1, batch, MEM_HEADS, mlen, MEM_HEAD_DIM), lambda i: (i, 0, 0, 0, 0)),
        ],
        out_shape=[
            jax.ShapeDtypeStruct((depth, batch, MEM_HEADS, MEM_HEAD_DIM, mlen), BF16),
            jax.ShapeDtypeStruct((depth, batch, MEM_HEADS, mlen, MEM_HEAD_DIM), BF16),
        ],
        compiler_params=_params(1),
        name="mem_kv",
    )(mem.reshape(batch * mlen, d), mem_norm_g.reshape(1, d), w_kv_bf16,
      k_norm_g.reshape(depth, 1, MEM_HEAD_DIM))


def _in_proj_kernel(x_ref, g_ref, w_ref, hg_ref, *out_refs, segs, norm_groups, chunk):
    xn = _rms_rows(x_ref[...], g_ref[...]).astype(BF16)
    for (c0, width), o_ref in zip(segs, out_refs):
        for cc in range(0, width, chunk):
            cw = min(chunk, width - cc)
            acc = jnp.dot(xn, w_ref[:, c0 + cc:c0 + cc + cw], preferred_element_type=F32)
            for gi in range(cw // LANE):
                col = c0 + cc + gi * LANE
                y = acc[:, gi * LANE:(gi + 1) * LANE]
                if col // LANE in norm_groups:
                    y = _rms_rows(y, hg_ref[:, col:col + LANE])
                o_ref[:, cc + gi * LANE:cc + (gi + 1) * LANE] = y.astype(o_ref.dtype)


def _in_proj(x2d, norm_g, w_bf16, head_gain, segs, out_dtypes, norm_groups, name):
    t, d = x2d.shape
    n = w_bf16.shape[1]
    tm = min(TOKEN_TILE, t)
    kern = functools.partial(_in_proj_kernel, segs=segs, norm_groups=frozenset(norm_groups),
                             chunk=4 * LANE)
    return pl.pallas_call(
        kern,
        grid=(t // tm,),
        in_specs=[
            pl.BlockSpec((tm, d), lambda i: (i, 0)),
            _resident((1, d)),
            _resident((d, n)),
            _resident((1, n)),
        ],
        out_specs=[pl.BlockSpec((tm, w), lambda i: (i, 0)) for (_, w) in segs],
        out_shape=[jax.ShapeDtypeStruct((t, w), dt) for (_, w), dt in zip(segs, out_dtypes)],
        compiler_params=_params(1),
        name=name,
    )(x2d, norm_g.reshape(1, d), w_bf16, head_gain)


def _ssd_kernel(z_ref, xbc_ref, dt_ref, cw_ref, cb_ref, dtb_ref, alog_ref, dexp_ref, ng_ref,
                o_ref, cbuf, state):
    L = SSD_CHUNK
    gw = SSD_HPG * SSD_HEAD_DIM
    c = pl.program_id(1)

    @pl.when(c == 0)
    def _():
        cbuf[0:SUBLANE, :] = jnp.zeros((SUBLANE, SSD_CONV_DIM), F32)
        state[...] = jnp.zeros(state.shape, F32)

    cbuf[SUBLANE:SUBLANE + L, :] = xbc_ref[...].astype(F32)
    acc = jnp.broadcast_to(cb_ref[...], (L, SSD_CONV_DIM))
    for k in range(SSD_CONV):
        acc = acc + cw_ref[k:k + 1, :] * cbuf[pl.ds(SUBLANE - (SSD_CONV - 1) + k, L), :]
    cbuf[0:SUBLANE, :] = cbuf[L:L + SUBLANE, :]
    u = acc * jax.nn.sigmoid(acc)
    xs = u[:, :SSD_INNER]
    bm = u[:, SSD_INNER:SSD_INNER + SSD_BC]
    cm = u[:, SSD_INNER + SSD_BC:]

    lane = lax.broadcasted_iota(jnp.int32, (1, LANE), 1)
    traw = dt_ref[...] + dtb_ref[...]
    dt = jnp.maximum(traw, 0.0) + jnp.log1p(jnp.exp(-jnp.abs(traw)))
    a = jnp.where(lane < SSD_HEADS, -jnp.exp(alog_ref[...]), 0.0)
    da = dt * a
    ri = lax.broadcasted_iota(jnp.int32, (L, L), 0)
    ci = lax.broadcasted_iota(jnp.int32, (L, L), 1)
    causal = ri >= ci
    tri = causal.astype(F32)
    cs = jnp.dot(tri, da, precision=HIGHEST, preferred_element_type=F32)
    cs_t = cs.T
    cs_last = cs[L - 1:L, :]
    ecs = jnp.exp(cs)
    dte = jnp.exp(cs_last - cs)
    erow = lax.broadcasted_iota(jnp.int32, (LANE, SSD_INNER), 0)
    ecol = lax.broadcasted_iota(jnp.int32, (LANE, SSD_INNER), 1) // SSD_HEAD_DIM
    expand = (erow == ecol).astype(F32)
    dt_x = jnp.dot(dt, expand, precision=HIGHEST, preferred_element_type=F32)
    ecs_x = jnp.dot(ecs, expand, precision=HIGHEST, preferred_element_type=F32)
    dte_x = jnp.dot(dte, expand, precision=HIGHEST, preferred_element_type=F32)
    xdt = xs * dt_x
    xdt_b = xdt.astype(BF16)
    xdte_b = (xdt * dte_x).astype(BF16)
    cdec_x = ecs_x[L - 1:L, :]
    lo_half = lax.broadcasted_iota(jnp.int32, (L, LANE), 1) < SSD_HEAD_DIM

    z = z_ref[...].astype(F32)
    gate = z * jax.nn.sigmoid(z)
    for g in range(SSD_GROUPS):
        g0 = g * gw
        cg = cm[:, g * SSD_STATE:(g + 1) * SSD_STATE].astype(BF16)
        bg_t = bm[:, g * SSD_STATE:(g + 1) * SSD_STATE].T.astype(BF16)
        cbm = jnp.dot(cg, bg_t, preferred_element_type=F32)
        st = state[g]
        y_g = jnp.dot(cg, st.astype(BF16), preferred_element_type=F32) * ecs_x[:, g0:g0 + gw]
        st_c = jnp.dot(bg_t, xdte_b[:, g0:g0 + gw], preferred_element_type=F32)
        state[g] = st * cdec_x[:, g0:g0 + gw] + st_c
        pairs = []
        for pr in range(SSD_HPG // 2):
            p0 = g0 + pr * LANE
            xp = xdt_b[:, p0:p0 + LANE]
            yd = None
            for half in range(2):
                h = g * SSD_HPG + pr * 2 + half
                seg = jnp.broadcast_to(cs[:, h:h + 1], (L, L)) - cs_t[h:h + 1, :]
                dec = jnp.where(causal, jnp.exp(seg), 0.0)
                wm = (cbm * dec).astype(BF16)
                keep = lo_half if half == 0 else jnp.logical_not(lo_half)
                part = jnp.dot(wm, jnp.where(keep, xp, jnp.zeros_like(xp)),
                               preferred_element_type=F32)
                yd = part if yd is None else yd + part
            pairs.append(yd)
        y_g = y_g + jnp.concatenate(pairs, axis=1) + dexp_ref[:, g0:g0 + gw] * xs[:, g0:g0 + gw]
        y_g = y_g * gate[:, g0:g0 + gw]
        o_ref[:, g0:g0 + gw] = _rms_rows(y_g, ng_ref[:, g0:g0 + gw]).astype(o_ref.dtype)


def _ssd(z, xbc, dt, conv_w, conv_b, dt_bias, a_log, d_skip, norm_g, batch, seq):
    t = batch * seq
    nc = seq // SSD_CHUNK
    L = SSD_CHUNK
    pad = LANE - SSD_HEADS
    row = lambda b, c: (b * nc + c, 0)
    return pl.pallas_call(
        _ssd_kernel,
        grid=(batch, nc),
        in_specs=[
            pl.BlockSpec((L, SSD_INNER), row),
            pl.BlockSpec((L, SSD_CONV_DIM), row),
            pl.BlockSpec((L, LANE), row),
            _resident((SSD_CONV, SSD_CONV_DIM)),
            _resident((1, SSD_CONV_DIM)),
            _resident((1, LANE)),
            _resident((1, LANE)),
            _resident((1, SSD_INNER)),
            _resident((1, SSD_INNER)),
        ],
        out_specs=pl.BlockSpec((L, SSD_INNER), row),
        out_shape=jax.ShapeDtypeStruct((t, SSD_INNER), BF16),
        scratch_shapes=[
            pltpu.VMEM((L + 2 * SUBLANE, SSD_CONV_DIM), F32),
            pltpu.VMEM((SSD_GROUPS, SSD_STATE, SSD_HPG * SSD_HEAD_DIM), F32),
        ],
        compiler_params=_params(2),
        name="ssd_mix",
    )(z, xbc, dt, conv_w, conv_b.reshape(1, -1),
      jnp.pad(dt_bias, (0, pad)).reshape(1, LANE), jnp.pad(a_log, (0, pad)).reshape(1, LANE),
      jnp.repeat(d_skip, SSD_HEAD_DIM).reshape(1, SSD_INNER), norm_g.reshape(1, SSD_INNER))


def _sb_kernel(drow_ref, dkey_ref, brow_ref, bkey_ref, q_ref, k_ref, v_ref, o_ref,
               acc_ref, carry_ref, n_ref, lm_ref, inc_ref, w_ref, *, kb, chunk, n_diag, n_bulk):
    acc_ref[...] = jnp.zeros(acc_ref.shape, F32)
    carry_ref[...] = jnp.zeros(carry_ref.shape, F32)
    tj = lax.broadcasted_iota(jnp.int32, (kb, kb), 0)
    ts = lax.broadcasted_iota(jnp.int32, (kb, kb), 1)
    suffix = (tj >= ts).astype(BF16)
    sign = jnp.uint32(0x80000000)

    def run(n_pairs, m, masked, row_ref, key_ref):
        if masked:
            col_minus_row = (lax.broadcasted_iota(jnp.int32, (chunk, kb), 1)
                             - lax.broadcasted_iota(jnp.int32, (chunk, kb), 0))

        def offsets(f):
            return pl.multiple_of(row_ref[f], kb), pl.multiple_of(key_ref[f], kb)

        def scores(f, s):
            r, k = offsets(f)
            n_ref[s, 0:m, :] = lax.dot_general(
                q_ref[pl.ds(r, m), :], k_ref[pl.ds(k, kb), :], (((1,), (1,)), ((), ())),
                preferred_element_type=F32)

        def log_terms(f, s):
            r, k = offsets(f)
            for c0 in range(0, m, chunk):
                n = n_ref[s, c0:c0 + chunk, :]
                neg_abs = pltpu.bitcast(pltpu.bitcast(n, jnp.uint32) | sign, F32)
                soft = jnp.log(1.0 + jnp.exp2(neg_abs)) * LOG2E
                l1m = jnp.minimum(n, 0.0) - soft
                if masked:
                    l1m = jnp.where(col_minus_row < r - k + c0, l1m, 0.0)
                lm_ref[s, c0:c0 + chunk, :] = l1m.astype(BF16)

        def suffix_sums(f, s):
            inc_ref[s, 0:m, :] = jnp.dot(lm_ref[s, 0:m, :], suffix, preferred_element_type=F32)

        def weights(f, s):
            r, k = offsets(f)
            for c0 in range(0, m, chunk):
                incl = inc_ref[s, c0:c0 + chunk, :]
                carry = carry_ref[pl.ds(r + c0, chunk), :]
                e = (incl - n_ref[s, c0:c0 + chunk, :]) + jnp.concatenate(
                    [carry] * (kb // LANE), axis=1)
                w = jnp.exp2(e)
                if masked:
                    w = jnp.where(col_minus_row < r - k + c0, w, 0.0)
                w_ref[s, c0:c0 + chunk, :] = w.astype(BF16)
                carry_ref[pl.ds(r + c0, chunk), :] = carry + incl[:, 0:1]

        def weighted_values(f, s):
            r, k = offsets(f)
            acc_ref[pl.ds(r, m), :] += jnp.dot(w_ref[s, 0:m, :], v_ref[pl.ds(k, kb), :],
                                               preferred_element_type=F32)

        stages = (scores, log_terms, suffix_sums, weights, weighted_values)
        depth = len(stages)

        def step(sigma, phase, first, last):
            for d in reversed(range(depth)):
                if first <= d <= last:
                    stages[d](sigma - d, (phase - d) % SB_SLOTS)

        def static_step(sigma):
            step(sigma, sigma % SB_SLOTS, max(0, sigma - n_pairs + 1), min(depth - 1, sigma))

        n_loop = max(0, (n_pairs - SB_SLOTS) // SB_SLOTS)
        loop_start = SB_SLOTS
        assert loop_start >= depth - 1
        for sigma in range(min(loop_start, n_pairs + depth - 1)):
            static_step(sigma)
        if n_loop > 0:
            def body(i, c):
                for phase in range(SB_SLOTS):
                    step(loop_start + i * SB_SLOTS + phase, phase, 0, depth - 1)
                return c
            lax.fori_loop(0, n_loop, body, 0)
        for sigma in range(loop_start + n_loop * SB_SLOTS, n_pairs + depth - 1):
            static_step(sigma)

    run(n_diag, kb, True, drow_ref, dkey_ref)
    run(n_bulk, 2 * kb, False, brow_ref, bkey_ref)
    o_ref[...] = acc_ref[...].astype(o_ref.dtype)


def _sb_pair_tables(seq, kb):
    diag, bulk = [], []
    for t in range(seq // (2 * kb)):
        lo, hi = 2 * t * kb, (2 * t + 1) * kb
        diag += [(lo, lo), (hi, hi), (hi, lo)]
        bulk += [(lo, j * kb) for j in reversed(range(2 * t))]
    as_i32 = lambda xs: jnp.asarray(xs, jnp.int32)
    return (as_i32([p[0] for p in diag]), as_i32([p[1] for p in diag]),
            as_i32([p[0] for p in bulk] or [0]), as_i32([p[1] for p in bulk] or [0]),
            len(diag), len(bulk))


def _sb_attn(qkv, batch, seq):
    t = batch * seq
    kb = SB_KEY_BLOCK
    assert seq % (4 * kb) == 0
    dh = SB_HEAD_DIM
    drow, dkey, brow, bkey, n_diag, n_bulk = _sb_pair_tables(seq, kb)
    kern = functools.partial(_sb_kernel, kb=kb, chunk=SB_ROW_CHUNK, n_diag=n_diag, n_bulk=n_bulk)
    ring = (SB_SLOTS, 2 * kb, kb)
    grid_spec = pltpu.PrefetchScalarGridSpec(
        num_scalar_prefetch=4,
        grid=(batch, SB_HEADS),
        in_specs=[
            pl.BlockSpec((seq, dh), lambda b, h, *_: (b, h)),
            pl.BlockSpec((seq, dh), lambda b, h, *_: (b, SB_HEADS + h)),
            pl.BlockSpec((seq, dh), lambda b, h, *_: (b, 2 * SB_HEADS + h)),
        ],
        out_specs=pl.BlockSpec((seq, dh), lambda b, h, *_: (b, h)),
        scratch_shapes=[
            pltpu.VMEM((seq, dh), F32),
            pltpu.VMEM((seq, LANE), F32),
            pltpu.VMEM(ring, F32),
            pltpu.VMEM(ring, BF16),
            pltpu.VMEM(ring, F32),
            pltpu.VMEM(ring, BF16),
        ],
    )
    return pl.pallas_call(
        kern,
        grid_spec=grid_spec,
        out_shape=jax.ShapeDtypeStruct((t, SB_WIDTH), BF16),
        compiler_params=_params(2),
        name="sb_attn",
    )(drow, dkey, brow, bkey, qkv, qkv, qkv)


def _out_kernel(x_ref, y_ref, qm_ref, kt_ref, v_ref, w_ref, o_ref):
    qm = qm_ref[...]
    heads = []
    for h in range(MEM_HEADS):
        s = jnp.dot(qm[:, h * LANE:(h + 1) * LANE], kt_ref[0, h], preferred_element_type=F32)
        m = jnp.max(s, axis=-1, keepdims=True)
        p = jnp.exp2(s - m)
        l = jnp.sum(p, axis=-1, keepdims=True)
        o = jnp.dot(p.astype(BF16), v_ref[0, h], preferred_element_type=F32) / l
        heads.append(o.astype(BF16))
    om = jnp.concatenate(heads, axis=1)
    ny = y_ref.shape[1]
    acc = x_ref[...] + jnp.dot(y_ref[...], w_ref[:ny, :], preferred_element_type=F32)
    o_ref[...] = acc + jnp.dot(om, w_ref[ny:, :], preferred_element_type=F32)


def _out_proj(x2d, y, qm_arr, qm_col_block, mem_kt, mem_v, w_out_bf16, seq):
    t, d = x2d.shape
    tm = min(TOKEN_TILE, seq)
    per_batch = seq // tm
    ny = y.shape[1]
    mlen = mem_v.shape[2]
    return pl.pallas_call(
        _out_kernel,
        grid=(t // tm,),
        in_specs=[
            pl.BlockSpec((tm, d), lambda i: (i, 0)),
            pl.BlockSpec((tm, ny), lambda i: (i, 0)),
            pl.BlockSpec((tm, MEM_WIDTH), lambda i: (i, qm_col_block)),
            pl.BlockSpec((1, MEM_HEADS, MEM_HEAD_DIM, mlen), lambda i: (i // per_batch, 0, 0, 0)),
            pl.BlockSpec((1, MEM_HEADS, mlen, MEM_HEAD_DIM), lambda i: (i // per_batch, 0, 0, 0)),
            _resident((MIX_WIDTH, d)),
        ],
        out_specs=pl.BlockSpec((tm, d), lambda i: (i, 0)),
        out_shape=jax.ShapeDtypeStruct((t, d), F32),
        compiler_params=_params(1),
        name="out_proj",
    )(x2d, y, qm_arr, mem_kt, mem_v, w_out_bf16)


def _ffn_kernel(x_ref, g_ref, wgu_ref, wd_ref, o_ref, *, hidden, chunk):
    x = x_ref[...]
    xn = _rms_rows(x, g_ref[...]).astype(BF16)
    acc = x
    for c0 in range(0, hidden, chunk):
        gt = jnp.dot(xn, wgu_ref[:, c0:c0 + chunk], preferred_element_type=F32)
        up = jnp.dot(xn, wgu_ref[:, hidden + c0:hidden + c0 + chunk], preferred_element_type=F32)
        hh = (gt * jax.nn.sigmoid(gt) * up).astype(BF16)
        acc = acc + jnp.dot(hh, wd_ref[c0:c0 + chunk, :], preferred_element_type=F32)
    o_ref[...] = acc


def _ffn(x2d, norm_g, wgu_bf16, wd_bf16):
    t, d = x2d.shape
    hidden = wd_bf16.shape[0]
    tm = min(TOKEN_TILE, t)
    kern = functools.partial(_ffn_kernel, hidden=hidden, chunk=FFN_HIDDEN_CHUNK)
    return pl.pallas_call(
        kern,
        grid=(t // tm,),
        in_specs=[
            pl.BlockSpec((tm, d), lambda i: (i, 0)),
            _resident((1, d)),
            _resident((d, 2 * hidden)),
            _resident((hidden, d)),
        ],
        out_specs=pl.BlockSpec((tm, d), lambda i: (i, 0)),
        out_shape=jax.ShapeDtypeStruct((t, d), F32),
        compiler_params=_params(1),
        name="ffn",
    )(x2d, norm_g.reshape(1, d), wgu_bf16, wd_bf16)


def _ssd_layer(x2d, mix_g, w_in, conv_w, conv_b, dt_bias, a_log, d_skip, norm_g, w_out,
               mq_g, mem_kt, mem_v, batch, seq):
    xbc_end = SSD_INNER + SSD_CONV_DIM
    dt_end = xbc_end + SSD_HEADS
    w = jnp.concatenate(
        [w_in[:, :xbc_end], w_in[:, dt_end:], w_in[:, xbc_end:dt_end],
         jnp.zeros((w_in.shape[0], LANE - SSD_HEADS), w_in.dtype)], axis=1).astype(BF16)
    qm0 = xbc_end
    segs = ((0, SSD_INNER), (SSD_INNER, SSD_CONV_DIM), (qm0, MEM_WIDTH), (qm0 + MEM_WIDTH, LANE))
    gain = jnp.ones((w.shape[1],), F32).at[qm0:qm0 + MEM_WIDTH].set(
        jnp.tile(mq_g * (MEM_HEAD_DIM ** -0.5 * LOG2E), MEM_HEADS)).reshape(1, -1)
    groups = range(qm0 // LANE, (qm0 + MEM_WIDTH) // LANE)
    z, xbc, qm, dt = _in_proj(x2d, mix_g, w, gain, segs, (BF16, BF16, BF16, F32), groups,
                              "in_proj_ssd")
    y = _ssd(z, xbc, dt, conv_w, conv_b, dt_bias, a_log, d_skip, norm_g, batch, seq)
    return _out_proj(x2d, y, qm, 0, mem_kt, mem_v, w_out.astype(BF16), seq)


def _sb_layer(x2d, mix_g, w_in, q_g, k_g, w_out, mq_g, mem_kt, mem_v, batch, seq):
    n = w_in.shape[1]
    qm0 = 3 * SB_WIDTH
    gain = jnp.concatenate([
        jnp.tile(q_g * (-(SB_HEAD_DIM ** -0.5) * LOG2E), SB_HEADS),
        jnp.tile(k_g, SB_HEADS),
        jnp.ones((SB_WIDTH,), F32),
        jnp.tile(mq_g * (MEM_HEAD_DIM ** -0.5 * LOG2E), MEM_HEADS)]).reshape(1, n)
    groups = list(range(0, 2 * SB_HEADS)) + list(range(qm0 // LANE, n // LANE))
    (qkv,) = _in_proj(x2d, mix_g, w_in.astype(BF16), gain, ((0, n),), (BF16,), groups,
                      "in_proj_sb")
    o = _sb_attn(qkv, batch, seq)
    return _out_proj(x2d, o, qkv, qm0 // MEM_WIDTH, mem_kt, mem_v, w_out.astype(BF16), seq)


def kernel(x, mem, mix_norm_g, ffn_norm_g, mem_norm_g, mem_w_kv, mem_q_norm_g, mem_k_norm_g, ssd_w_in, ssd_conv_w, ssd_conv_b, ssd_dt_bias, ssd_a_log, ssd_d, ssd_norm_g, ssd_w_out, sb_w_in, sb_q_norm_g, sb_k_norm_g, sb_w_out, ffn_w_gate_up, ffn_w_down):
    batch, seq, d = x.shape
    x2d = x.reshape(batch * seq, d)
    mem_kt, mem_v = _mem_kv(mem, mem_norm_g, mem_w_kv.astype(BF16), mem_k_norm_g)
    for i in range(DEPTH):
        j = i // 2
        if i % 2 == 0:
            x2d = _ssd_layer(x2d, mix_norm_g[i], ssd_w_in[j], ssd_conv_w[j], ssd_conv_b[j],
                             ssd_dt_bias[j], ssd_a_log[j], ssd_d[j], ssd_norm_g[j], ssd_w_out[j],
                             mem_q_norm_g[i], mem_kt[i], mem_v[i], batch, seq)
        else:
            x2d = _sb_layer(x2d, mix_norm_g[i], sb_w_in[j], sb_q_norm_g[j], sb_k_norm_g[j],
                            sb_w_out[j], mem_q_norm_g[i], mem_kt[i], mem_v[i], batch, seq)
        x2d = _ffn(x2d, ffn_norm_g[i], ffn_w_gate_up[i].astype(BF16), ffn_w_down[i].astype(BF16))
    return x2d.reshape(batch, seq, d)
```

```python
import functools
import math

import jax
import jax.numpy as jnp
from jax import lax
from jax.experimental import pallas as pl
from jax.experimental.pallas import tpu as pltpu

D_MODEL = 1024
DEPTH = 4
MIX_WIDTH = 2 * D_MODEL
EPS = 1e-6
MEM_HEADS = 4
MEM_WIDTH = MIX_WIDTH // 4
MEM_HEAD_DIM = MEM_WIDTH // MEM_HEADS
SSD_HEAD_DIM = 64
SSD_INNER = MIX_WIDTH - MEM_WIDTH
SSD_HEADS = SSD_INNER // SSD_HEAD_DIM
SSD_GROUPS = 4
SSD_HPG = SSD_HEADS // SSD_GROUPS
SSD_STATE = 128
SSD_CONV = 4
SSD_CHUNK = 128
SSD_BC = SSD_GROUPS * SSD_STATE
SSD_CONV_DIM = SSD_INNER + 2 * SSD_BC
SB_HEAD_DIM = 128
SB_WIDTH = MIX_WIDTH - MEM_WIDTH
SB_HEADS = SB_WIDTH // SB_HEAD_DIM
FFN_HIDDEN = 2816

LANE = 128
SUBLANE = 8
VMEM_LIMIT_BYTES = 56 * 1024 * 1024

LOG2E = 1.4426950408889634
F32 = jnp.float32
BF16 = jnp.bfloat16

TOKEN_TILE = 512
SB_KEY_BLOCK = 256
SB_SLOTS = 8
SB_ROW_CHUNK = 64
FFN_HIDDEN_CHUNK = 256


def _params(n_axes, flags=None):
    return pltpu.CompilerParams(
        dimension_semantics=("arbitrary",) * n_axes,
        vmem_limit_bytes=VMEM_LIMIT_BYTES,
        flags=flags,
    )


def _rms_rows(xf, gain):
    ms = jnp.mean(xf * xf, axis=-1, keepdims=True)
    return xf * lax.rsqrt(ms + EPS) * gain


def _split_bf16x3(x):
    top = jnp.uint32(0xFFFF0000)
    hi = pltpu.bitcast(pltpu.bitcast(x, jnp.uint32) & top, F32)
    r1 = x - hi
    mid = pltpu.bitcast(pltpu.bitcast(r1, jnp.uint32) & top, F32)
    lo = r1 - mid
    return hi.astype(BF16), mid.astype(BF16), lo.astype(BF16)


def _layer_block(stacked_shape, layer):
    rest = tuple(stacked_shape[1:])
    zeros = (0,) * len(rest)
    return pl.BlockSpec((None,) + rest, lambda *_: (layer,) + zeros,
                        pipeline_mode=pl.Buffered(1))


def _resident(shape):
    nd = len(shape)
    return pl.BlockSpec(shape, lambda *_: (0,) * nd, pipeline_mode=pl.Buffered(1))


def _memkv_kernel(mem_ref, g_ref, w_ref, kg_ref, kt_ref, v_ref, *, batch, mlen):
    mn = _rms_rows(mem_ref[...], g_ref[...]).astype(BF16)
    kv = jnp.dot(mn, w_ref[0], preferred_element_type=F32)
    kg = kg_ref[0]
    for b in range(batch):
        for h in range(MEM_HEADS):
            k = kv[b * mlen:(b + 1) * mlen, h * LANE:(h + 1) * LANE]
            k = _rms_rows(k, kg)
            kt_ref[0, b, h] = k.T.astype(BF16)
            c0 = MEM_WIDTH + h * LANE
            v_ref[0, b, h] = kv[b * mlen:(b + 1) * mlen, c0:c0 + LANE].astype(BF16)


def _mem_kv(mem, mem_norm_g, w_kv_bf16, k_norm_g):
    batch, mlen, d = mem.shape
    depth = w_kv_bf16.shape[0]
    kern = functools.partial(_memkv_kernel, batch=batch, mlen=mlen)
    return pl.pallas_call(
        kern,
        grid=(depth,),
        in_specs=[
            pl.BlockSpec((batch * mlen, d), lambda i: (0, 0)),
            pl.BlockSpec((1, d), lambda i: (0, 0)),
            pl.BlockSpec((1, d, 2 * MEM_WIDTH), lambda i: (i, 0, 0)),
            pl.BlockSpec((1, 1, MEM_HEAD_DIM), lambda i: (i, 0, 0)),
        ],
        out_specs=[
            pl.BlockSpec((1, batch, MEM_HEADS, MEM_HEAD_DIM, mlen), lambda i: (i, 0, 0, 0, 0)),
            pl.BlockSpec((1, batch, MEM_HEADS, mlen, MEM_HEAD_DIM), lambda i: (i, 0, 0, 0, 0)),
        ],
        out_shape=[
            jax.ShapeDtypeStruct((depth, batch, MEM_HEADS, MEM_HEAD_DIM, mlen), BF16),
            jax.ShapeDtypeStruct((depth, batch, MEM_HEADS, mlen, MEM_HEAD_DIM), BF16),
        ],
        compiler_params=_params(1),
        name="mem_kv",
    )(mem.reshape(batch * mlen, d), mem_norm_g.reshape(1, d), w_kv_bf16,
      k_norm_g.reshape(depth, 1, MEM_HEAD_DIM))


def _in_proj_kernel(x_ref, g_ref, w_ref, hg_ref, *out_refs, segs, norm_groups, chunk):
    xn = _rms_rows(x_ref[...], g_ref[...]).astype(BF16)
    for (c0, width), o_ref in zip(segs, out_refs):
        for cc in range(0, width, chunk):
            cw = min(chunk, width - cc)
            acc = jnp.dot(xn, w_ref[:, c0 + cc:c0 + cc + cw], preferred_element_type=F32)
            for gi in range(cw // LANE):
                col = c0 + cc + gi * LANE
                y = acc[:, gi * LANE:(gi + 1) * LANE]
                if col // LANE in norm_groups:
                    y = _rms_rows(y, hg_ref[:, col:col + LANE])
                o_ref[:, cc + gi * LANE:cc + (gi + 1) * LANE] = y.astype(o_ref.dtype)


def _in_proj(x2d, norm_g, w_stack, layer, head_gain, segs, out_dtypes, norm_groups, name):
    t, d = x2d.shape
    n = w_stack.shape[2]
    tm = min(TOKEN_TILE, t)
    kern = functools.partial(_in_proj_kernel, segs=segs, norm_groups=frozenset(norm_groups),
                             chunk=4 * LANE)
    return pl.pallas_call(
        kern,
        grid=(t // tm,),
        in_specs=[
            pl.BlockSpec((tm, d), lambda i: (i, 0)),
            _resident((1, d)),
            _layer_block(w_stack.shape, layer),
            _resident((1, n)),
        ],
        out_specs=[pl.BlockSpec((tm, w), lambda i: (i, 0)) for (_, w) in segs],
        out_shape=[jax.ShapeDtypeStruct((t, w), dt) for (_, w), dt in zip(segs, out_dtypes)],
        compiler_params=_params(1),
        name=name,
    )(x2d, norm_g.reshape(1, d), w_stack, head_gain)


def _ssd_kernel(z_ref, xbc_ref, dt_ref, cw_ref, cb_ref, dtb_ref, alog_ref, dexp_ref, ng_ref,
                o_ref, cbuf, state):
    L = SSD_CHUNK
    gw = SSD_HPG * SSD_HEAD_DIM
    c = pl.program_id(1)

    @pl.when(c == 0)
    def _():
        cbuf[0:SUBLANE, :] = jnp.zeros((SUBLANE, SSD_CONV_DIM), F32)
        state[...] = jnp.zeros(state.shape, F32)

    cbuf[SUBLANE:SUBLANE + L, :] = xbc_ref[...].astype(F32)
    acc = jnp.broadcast_to(cb_ref[...], (L, SSD_CONV_DIM))
    for k in range(SSD_CONV):
        acc = acc + cw_ref[k:k + 1, :] * cbuf[pl.ds(SUBLANE - (SSD_CONV - 1) + k, L), :]
    cbuf[0:SUBLANE, :] = cbuf[L:L + SUBLANE, :]
    u = acc * jax.nn.sigmoid(acc)
    xs = u[:, :SSD_INNER]
    bm = u[:, SSD_INNER:SSD_INNER + SSD_BC]
    cm = u[:, SSD_INNER + SSD_BC:]

    lane = lax.broadcasted_iota(jnp.int32, (1, LANE), 1)
    traw = dt_ref[...] + dtb_ref[...]
    dt = jnp.maximum(traw, 0.0) + jnp.log1p(jnp.exp(-jnp.abs(traw)))
    a = jnp.where(lane < SSD_HEADS, -jnp.exp(alog_ref[...]), 0.0)
    da = dt * a
    ri = lax.broadcasted_iota(jnp.int32, (L, L), 0)
    ci = lax.broadcasted_iota(jnp.int32, (L, L), 1)
    causal = ri >= ci
    da3 = jnp.dot(causal.astype(BF16), jnp.concatenate(_split_bf16x3(da), axis=1),
                  preferred_element_type=F32)
    cs = (da3[:, :LANE] + da3[:, LANE:2 * LANE]) + da3[:, 2 * LANE:]
    cs_t = cs.T
    cs_last = cs[L - 1:L, :]
    ecs = jnp.exp(cs)
    dte = jnp.exp(cs_last - cs)
    erow = lax.broadcasted_iota(jnp.int32, (LANE, SSD_INNER), 0)
    ecol = lax.broadcasted_iota(jnp.int32, (LANE, SSD_INNER), 1) // SSD_HEAD_DIM
    expand = (erow == ecol).astype(BF16)
    stacked = jnp.concatenate([dt, ecs, dte], axis=0)
    expanded = jnp.dot(jnp.concatenate(_split_bf16x3(stacked), axis=1),
                       jnp.concatenate([expand] * 3, axis=0), preferred_element_type=F32)
    dt_x, ecs_x, dte_x = expanded[:L], expanded[L:2 * L], expanded[2 * L:]
    xdt = xs * dt_x
    xdt_b = xdt.astype(BF16)
    xdte_b = (xdt * dte_x).astype(BF16)
    cdec_x = ecs_x[L - 1:L, :]
    lo_half = lax.broadcasted_iota(jnp.int32, (L, LANE), 1) < SSD_HEAD_DIM

    z = z_ref[...].astype(F32)
    gate = z * jax.nn.sigmoid(z)
    for g in range(SSD_GROUPS):
        g0 = g * gw
        cg = cm[:, g * SSD_STATE:(g + 1) * SSD_STATE].astype(BF16)
        bg_t = bm[:, g * SSD_STATE:(g + 1) * SSD_STATE].T.astype(BF16)
        cbm = jnp.dot(cg, bg_t, preferred_element_type=F32)
        st = state[g]
        y_g = jnp.dot(cg, st.astype(BF16), preferred_element_type=F32) * ecs_x[:, g0:g0 + gw]
        st_c = jnp.dot(bg_t, xdte_b[:, g0:g0 + gw], preferred_element_type=F32)
        state[g] = st * cdec_x[:, g0:g0 + gw] + st_c
        pairs = []
        for pr in range(SSD_HPG // 2):
            p0 = g0 + pr * LANE
            xp = xdt_b[:, p0:p0 + LANE]
            yd = None
            for half in range(2):
                h = g * SSD_HPG + pr * 2 + half
                seg = jnp.broadcast_to(cs[:, h:h + 1], (L, L)) - cs_t[h:h + 1, :]
                dec = jnp.where(causal, jnp.exp(seg), 0.0)
                wm = (cbm * dec).astype(BF16)
                keep = lo_half if half == 0 else jnp.logical_not(lo_half)
                part = jnp.dot(wm, jnp.where(keep, xp, jnp.zeros_like(xp)),
                               preferred_element_type=F32)
                yd = part if yd is None else yd + part
            pairs.append(yd)
        y_g = y_g + jnp.concatenate(pairs, axis=1) + dexp_ref[:, g0:g0 + gw] * xs[:, g0:g0 + gw]
        y_g = y_g * gate[:, g0:g0 + gw]
        o_ref[:, g0:g0 + gw] = _rms_rows(y_g, ng_ref[:, g0:g0 + gw]).astype(o_ref.dtype)


def _ssd(z, xbc, dt, conv_w, conv_b, dt_bias, a_log, d_skip, norm_g, batch, seq):
    t = batch * seq
    nc = seq // SSD_CHUNK
    L = SSD_CHUNK
    pad = LANE - SSD_HEADS
    row = lambda b, c: (b * nc + c, 0)
    return pl.pallas_call(
        _ssd_kernel,
        grid=(batch, nc),
        in_specs=[
            pl.BlockSpec((L, SSD_INNER), row),
            pl.BlockSpec((L, SSD_CONV_DIM), row),
            pl.BlockSpec((L, LANE), row),
            _resident((SSD_CONV, SSD_CONV_DIM)),
            _resident((1, SSD_CONV_DIM)),
            _resident((1, LANE)),
            _resident((1, LANE)),
            _resident((1, SSD_INNER)),
            _resident((1, SSD_INNER)),
        ],
        out_specs=pl.BlockSpec((L, SSD_INNER), row),
        out_shape=jax.ShapeDtypeStruct((t, SSD_INNER), BF16),
        scratch_shapes=[
            pltpu.VMEM((L + 2 * SUBLANE, SSD_CONV_DIM), F32),
            pltpu.VMEM((SSD_GROUPS, SSD_STATE, SSD_HPG * SSD_HEAD_DIM), F32),
        ],
        compiler_params=_params(2),
        name="ssd_mix",
    )(z, xbc, dt, conv_w, conv_b.reshape(1, -1),
      jnp.pad(dt_bias, (0, pad)).reshape(1, LANE), jnp.pad(a_log, (0, pad)).reshape(1, LANE),
      jnp.repeat(d_skip, SSD_HEAD_DIM).reshape(1, SSD_INNER), norm_g.reshape(1, SSD_INNER))


def _sb_kernel(drow_ref, dkey_ref, brow_ref, bkey_ref, q_ref, k_ref, v_ref, o_ref,
               acc_ref, carry_ref, n_ref, lm_ref, inc_ref, w_ref, *, kb, chunk, n_diag, n_bulk):
    acc_ref[...] = jnp.zeros(acc_ref.shape, F32)
    carry_ref[...] = jnp.zeros(carry_ref.shape, F32)
    tj = lax.broadcasted_iota(jnp.int32, (kb, kb), 0)
    ts = lax.broadcasted_iota(jnp.int32, (kb, kb), 1)
    suffix = (tj >= ts).astype(BF16)
    sign = jnp.uint32(0x80000000)
    sign16 = jnp.uint16(0x8000)

    def run(n_pairs, m, masked, row_ref, key_ref):
        if masked:
            col_minus_row = (lax.broadcasted_iota(jnp.int32, (chunk, kb), 1)
                             - lax.broadcasted_iota(jnp.int32, (chunk, kb), 0))

        def offsets(f):
            return pl.multiple_of(row_ref[f], kb), pl.multiple_of(key_ref[f], kb)

        def scores(f, s):
            r, k = offsets(f)
            n_ref[s, 0:m, :] = lax.dot_general(
                q_ref[pl.ds(r, m), :], k_ref[pl.ds(k, kb), :], (((1,), (1,)), ((), ())),
                preferred_element_type=F32)

        def log_terms(f, s):
            r, k = offsets(f)
            for c0 in range(0, m, chunk):
                n = n_ref[s, c0:c0 + chunk, :]
                if masked:
                    neg_abs = pltpu.bitcast(pltpu.bitcast(n, jnp.uint32) | sign, F32)
                    soft = jnp.log(1.0 + jnp.exp2(neg_abs)) * LOG2E
                    l1m = jnp.minimum(n, 0.0) - soft
                    l1m = jnp.where(col_minus_row < r - k + c0, l1m, 0.0)
                    lm_ref[s, c0:c0 + chunk, :] = l1m.astype(BF16)
                else:
                    nb = n.astype(BF16)
                    neg_abs = pltpu.bitcast(pltpu.bitcast(nb, jnp.uint16) | sign16, BF16)
                    soft = jnp.log(1.0 + jnp.exp2(neg_abs)) * LOG2E
                    lm_ref[s, c0:c0 + chunk, :] = jnp.minimum(nb, 0.0) - soft

        def suffix_sums(f, s):
            inc_ref[s, 0:m, :] = jnp.dot(lm_ref[s, 0:m, :], suffix, preferred_element_type=F32)

        def weights(f, s):
            r, k = offsets(f)
            for c0 in range(0, m, chunk):
                incl = inc_ref[s, c0:c0 + chunk, :]
                carry = carry_ref[pl.ds(r + c0, chunk), :]
                e = (incl - n_ref[s, c0:c0 + chunk, :]) + jnp.concatenate(
                    [carry] * (kb // LANE), axis=1)
                w = jnp.exp2(e)
                if masked:
                    w = jnp.where(col_minus_row < r - k + c0, w, 0.0)
                w_ref[s, c0:c0 + chunk, :] = w.astype(BF16)
                carry_ref[pl.ds(r + c0, chunk), :] = carry + incl[:, 0:1]

        def weighted_values(f, s):
            r, k = offsets(f)
            acc_ref[pl.ds(r, m), :] += jnp.dot(w_ref[s, 0:m, :], v_ref[pl.ds(k, kb), :],
                                               preferred_element_type=F32)

        stages = (scores, log_terms, suffix_sums, weights, weighted_values)
        depth = len(stages)

        def step(sigma, phase, first, last):
            for d in reversed(range(depth)):
                if first <= d <= last:
                    stages[d](sigma - d, (phase - d) % SB_SLOTS)

        def static_step(sigma):
            step(sigma, sigma % SB_SLOTS, max(0, sigma - n_pairs + 1), min(depth - 1, sigma))

        n_loop = max(0, (n_pairs - SB_SLOTS) // SB_SLOTS)
        loop_start = SB_SLOTS
        assert loop_start >= depth - 1
        for sigma in range(min(loop_start, n_pairs + depth - 1)):
            static_step(sigma)
        if n_loop > 0:
            def body(i, c):
                for phase in range(SB_SLOTS):
                    step(loop_start + i * SB_SLOTS + phase, phase, 0, depth - 1)
                return c
            lax.fori_loop(0, n_loop, body, 0)
        for sigma in range(loop_start + n_loop * SB_SLOTS, n_pairs + depth - 1):
            static_step(sigma)

    run(n_diag, kb, True, drow_ref, dkey_ref)
    run(n_bulk, 2 * kb, False, brow_ref, bkey_ref)
    o_ref[...] = acc_ref[...].astype(o_ref.dtype)


def _sb_pair_tables(seq, kb):
    diag, bulk = [], []
    for t in range(seq // (2 * kb)):
        lo, hi = 2 * t * kb, (2 * t + 1) * kb
        diag += [(lo, lo), (hi, hi), (hi, lo)]
        bulk += [(lo, j * kb) for j in reversed(range(2 * t))]
    as_i32 = lambda xs: jnp.asarray(xs, jnp.int32)
    return (as_i32([p[0] for p in diag]), as_i32([p[1] for p in diag]),
            as_i32([p[0] for p in bulk] or [0]), as_i32([p[1] for p in bulk] or [0]),
            len(diag), len(bulk))


def _sb_attn(qkv, batch, seq):
    t = batch * seq
    kb = SB_KEY_BLOCK
    assert seq % (4 * kb) == 0
    dh = SB_HEAD_DIM
    drow, dkey, brow, bkey, n_diag, n_bulk = _sb_pair_tables(seq, kb)
    kern = functools.partial(_sb_kernel, kb=kb, chunk=SB_ROW_CHUNK, n_diag=n_diag, n_bulk=n_bulk)
    ring = (SB_SLOTS, 2 * kb, kb)
    grid_spec = pltpu.PrefetchScalarGridSpec(
        num_scalar_prefetch=4,
        grid=(batch, SB_HEADS),
        in_specs=[
            pl.BlockSpec((seq, dh), lambda b, h, *_: (b, h)),
            pl.BlockSpec((seq, dh), lambda b, h, *_: (b, SB_HEADS + h)),
            pl.BlockSpec((seq, dh), lambda b, h, *_: (b, 2 * SB_HEADS + h)),
        ],
        out_specs=pl.BlockSpec((seq, dh), lambda b, h, *_: (b, h)),
        scratch_shapes=[
            pltpu.VMEM((seq, dh), F32),
            pltpu.VMEM((seq, LANE), F32),
            pltpu.VMEM(ring, F32),
            pltpu.VMEM(ring, BF16),
            pltpu.VMEM(ring, F32),
            pltpu.VMEM(ring, BF16),
        ],
    )
    return pl.pallas_call(
        kern,
        grid_spec=grid_spec,
        out_shape=jax.ShapeDtypeStruct((t, SB_WIDTH), BF16),
        compiler_params=_params(2),
        name="sb_attn",
    )(drow, dkey, brow, bkey, qkv, qkv, qkv)


def _out_kernel(x_ref, y_ref, qm_ref, kt_ref, v_ref, w_ref, o_ref):
    qm = qm_ref[...]
    heads = []
    for h in range(MEM_HEADS):
        s = jnp.dot(qm[:, h * LANE:(h + 1) * LANE], kt_ref[0, h], preferred_element_type=F32)
        m = jnp.max(s, axis=-1, keepdims=True)
        p = jnp.exp2(s - m)
        l = jnp.sum(p, axis=-1, keepdims=True)
        o = jnp.dot(p.astype(BF16), v_ref[0, h], preferred_element_type=F32) / l
        heads.append(o.astype(BF16))
    om = jnp.concatenate(heads, axis=1)
    ny = y_ref.shape[1]
    acc = x_ref[...] + jnp.dot(y_ref[...], w_ref[:ny, :], preferred_element_type=F32)
    o_ref[...] = acc + jnp.dot(om, w_ref[ny:, :], preferred_element_type=F32)


def _out_proj(x2d, y, qm_arr, qm_col_block, mem_kt, mem_v, mem_layer, w_out_stack, w_layer, seq):
    t, d = x2d.shape
    tm = min(TOKEN_TILE, seq)
    per_batch = seq // tm
    ny = y.shape[1]
    mlen = mem_v.shape[3]
    return pl.pallas_call(
        _out_kernel,
        grid=(t // tm,),
        in_specs=[
            pl.BlockSpec((tm, d), lambda i: (i, 0)),
            pl.BlockSpec((tm, ny), lambda i: (i, 0)),
            pl.BlockSpec((tm, MEM_WIDTH), lambda i: (i, qm_col_block)),
            pl.BlockSpec((None, 1, MEM_HEADS, MEM_HEAD_DIM, mlen),
                         lambda i: (mem_layer, i // per_batch, 0, 0, 0)),
            pl.BlockSpec((None, 1, MEM_HEADS, mlen, MEM_HEAD_DIM),
                         lambda i: (mem_layer, i // per_batch, 0, 0, 0)),
            _layer_block(w_out_stack.shape, w_layer),
        ],
        out_specs=pl.BlockSpec((tm, d), lambda i: (i, 0)),
        out_shape=jax.ShapeDtypeStruct((t, d), F32),
        compiler_params=_params(1),
        name="out_proj",
    )(x2d, y, qm_arr, mem_kt, mem_v, w_out_stack)


def _ffn_kernel(x_ref, g_ref, wgu_ref, wd_ref, o_ref, *, hidden, chunk):
    x = x_ref[...]
    xn = _rms_rows(x, g_ref[...]).astype(BF16)
    acc = x
    for c0 in range(0, hidden, chunk):
        gt = jnp.dot(xn, wgu_ref[:, c0:c0 + chunk], preferred_element_type=F32)
        up = jnp.dot(xn, wgu_ref[:, hidden + c0:hidden + c0 + chunk], preferred_element_type=F32)
        hh = (gt * jax.nn.sigmoid(gt) * up).astype(BF16)
        acc = acc + jnp.dot(hh, wd_ref[c0:c0 + chunk, :], preferred_element_type=F32)
    o_ref[...] = acc


def _ffn(x2d, norm_g, wgu_stack, wd_stack, layer):
    t, d = x2d.shape
    hidden = wd_stack.shape[1]
    tm = min(TOKEN_TILE, t)
    kern = functools.partial(_ffn_kernel, hidden=hidden, chunk=FFN_HIDDEN_CHUNK)
    return pl.pallas_call(
        kern,
        grid=(t // tm,),
        in_specs=[
            pl.BlockSpec((tm, d), lambda i: (i, 0)),
            _resident((1, d)),
            _layer_block(wgu_stack.shape, layer),
            _layer_block(wd_stack.shape, layer),
        ],
        out_specs=pl.BlockSpec((tm, d), lambda i: (i, 0)),
        out_shape=jax.ShapeDtypeStruct((t, d), F32),
        compiler_params=_params(1),
        name="ffn",
    )(x2d, norm_g.reshape(1, d), wgu_stack, wd_stack)


SSD_XBC_END = SSD_INNER + SSD_CONV_DIM


def _ssd_w_in_regrouped(w_in):
    dt_end = SSD_XBC_END + SSD_HEADS
    pad = jnp.zeros(w_in.shape[:2] + (LANE - SSD_HEADS,), w_in.dtype)
    return jnp.concatenate([w_in[..., :SSD_XBC_END], w_in[..., dt_end:],
                            w_in[..., SSD_XBC_END:dt_end], pad], axis=-1).astype(BF16)


def _ssd_layer(x2d, mix_g, w_in_stack, j, conv_w, conv_b, dt_bias, a_log, d_skip, norm_g,
               w_out_stack, mq_g, mem_kt, mem_v, mem_layer, batch, seq):
    qm0 = SSD_XBC_END
    segs = ((0, SSD_INNER), (SSD_INNER, SSD_CONV_DIM), (qm0, MEM_WIDTH), (qm0 + MEM_WIDTH, LANE))
    gain = jnp.ones((w_in_stack.shape[2],), F32).at[qm0:qm0 + MEM_WIDTH].set(
        jnp.tile(mq_g * (MEM_HEAD_DIM ** -0.5 * LOG2E), MEM_HEADS)).reshape(1, -1)
    groups = range(qm0 // LANE, (qm0 + MEM_WIDTH) // LANE)
    z, xbc, qm, dt = _in_proj(x2d, mix_g, w_in_stack, j, gain, segs, (BF16, BF16, BF16, F32),
                              groups, "in_proj_ssd")
    y = _ssd(z, xbc, dt, conv_w, conv_b, dt_bias, a_log, d_skip, norm_g, batch, seq)
    return _out_proj(x2d, y, qm, 0, mem_kt, mem_v, mem_layer, w_out_stack, j, seq)


def _sb_layer(x2d, mix_g, w_in_stack, j, q_g, k_g, w_out_stack, mq_g, mem_kt, mem_v, mem_layer,
              batch, seq):
    n = w_in_stack.shape[2]
    qm0 = 3 * SB_WIDTH
    gain = jnp.concatenate([
        jnp.tile(q_g * (-(SB_HEAD_DIM ** -0.5) * LOG2E), SB_HEADS),
        jnp.tile(k_g, SB_HEADS),
        jnp.ones((SB_WIDTH,), F32),
        jnp.tile(mq_g * (MEM_HEAD_DIM ** -0.5 * LOG2E), MEM_HEADS)]).reshape(1, n)
    groups = list(range(0, 2 * SB_HEADS)) + list(range(qm0 // LANE, n // LANE))
    (qkv,) = _in_proj(x2d, mix_g, w_in_stack, j, gain, ((0, n),), (BF16,), groups, "in_proj_sb")
    o = _sb_attn(qkv, batch, seq)
    return _out_proj(x2d, o, qkv, qm0 // MEM_WIDTH, mem_kt, mem_v, mem_layer, w_out_stack, j, seq)


def kernel(x, mem, mix_norm_g, ffn_norm_g, mem_norm_g, mem_w_kv, mem_q_norm_g, mem_k_norm_g, ssd_w_in, ssd_conv_w, ssd_conv_b, ssd_dt_bias, ssd_a_log, ssd_d, ssd_norm_g, ssd_w_out, sb_w_in, sb_q_norm_g, sb_k_norm_g, sb_w_out, ffn_w_gate_up, ffn_w_down):
    batch, seq, d = x.shape
    x2d = x.reshape(batch * seq, d)
    mem_kt, mem_v = _mem_kv(mem, mem_norm_g, mem_w_kv.astype(BF16), mem_k_norm_g)
    ssd_w_in_b = _ssd_w_in_regrouped(ssd_w_in)
    ssd_w_out_b = ssd_w_out.astype(BF16)
    sb_w_in_b = sb_w_in.astype(BF16)
    sb_w_out_b = sb_w_out.astype(BF16)
    wgu_b = ffn_w_gate_up.astype(BF16)
    wd_b = ffn_w_down.astype(BF16)
    for i in range(DEPTH):
        j = i // 2
        if i % 2 == 0:
            x2d = _ssd_layer(x2d, mix_norm_g[i], ssd_w_in_b, j, ssd_conv_w[j], ssd_conv_b[j],
                             ssd_dt_bias[j], ssd_a_log[j], ssd_d[j], ssd_norm_g[j], ssd_w_out_b,
                             mem_q_norm_g[i], mem_kt, mem_v, i, batch, seq)
        else:
            x2d = _sb_layer(x2d, mix_norm_g[i], sb_w_in_b, j, sb_q_norm_g[j], sb_k_norm_g[j],
                            sb_w_out_b, mem_q_norm_g[i], mem_kt, mem_v, i, batch, seq)
        x2d = _ffn(x2d, ffn_norm_g[i], wgu_b, wd_b, i)
    return x2d.reshape(batch, seq, d)
```

```python
import functools
import math

import jax
import jax.numpy as jnp
from jax import lax
from jax.experimental import pallas as pl
from jax.experimental.pallas import tpu as pltpu

D_MODEL = 1024
DEPTH = 4
MIX_WIDTH = 2 * D_MODEL
EPS = 1e-6
MEM_HEADS = 4
MEM_WIDTH = MIX_WIDTH // 4
MEM_HEAD_DIM = MEM_WIDTH // MEM_HEADS
SSD_HEAD_DIM = 64
SSD_INNER = MIX_WIDTH - MEM_WIDTH
SSD_HEADS = SSD_INNER // SSD_HEAD_DIM
SSD_GROUPS = 4
SSD_HPG = SSD_HEADS // SSD_GROUPS
SSD_STATE = 128
SSD_CONV = 4
SSD_CHUNK = 128
SSD_BC = SSD_GROUPS * SSD_STATE
SSD_CONV_DIM = SSD_INNER + 2 * SSD_BC
SB_HEAD_DIM = 128
SB_WIDTH = MIX_WIDTH - MEM_WIDTH
SB_HEADS = SB_WIDTH // SB_HEAD_DIM
FFN_HIDDEN = 2816

LANE = 128
SUBLANE = 8
VMEM_LIMIT_BYTES = 56 * 1024 * 1024

LOG2E = 1.4426950408889634
F32 = jnp.float32
BF16 = jnp.bfloat16

TOKEN_TILE = 512
SB_KEY_BLOCK = 256
SB_SLOTS = 8
SB_ROW_CHUNK = 64
FFN_HIDDEN_CHUNK = 256


def _params(n_axes, flags=None):
    return pltpu.CompilerParams(
        dimension_semantics=("arbitrary",) * n_axes,
        vmem_limit_bytes=VMEM_LIMIT_BYTES,
        flags=flags,
    )


def _rms_rows(xf, gain):
    ms = jnp.mean(xf * xf, axis=-1, keepdims=True)
    return xf * lax.rsqrt(ms + EPS) * gain


def _split_bf16x3(x):
    top = jnp.uint32(0xFFFF0000)
    hi = pltpu.bitcast(pltpu.bitcast(x, jnp.uint32) & top, F32)
    r1 = x - hi
    mid = pltpu.bitcast(pltpu.bitcast(r1, jnp.uint32) & top, F32)
    lo = r1 - mid
    return hi.astype(BF16), mid.astype(BF16), lo.astype(BF16)


def _layer_block(stacked_shape, layer):
    rest = tuple(stacked_shape[1:])
    zeros = (0,) * len(rest)
    return pl.BlockSpec((None,) + rest, lambda *_: (layer,) + zeros,
                        pipeline_mode=pl.Buffered(1))


def _resident(shape):
    nd = len(shape)
    return pl.BlockSpec(shape, lambda *_: (0,) * nd, pipeline_mode=pl.Buffered(1))


def _memkv_kernel(mem_ref, g_ref, w_ref, kg_ref, kt_ref, v_ref, *, batch, mlen):
    mn = _rms_rows(mem_ref[...], g_ref[...]).astype(BF16)
    kv = jnp.dot(mn, w_ref[0], preferred_element_type=F32)
    kg = kg_ref[0]
    for b in range(batch):
        for h in range(MEM_HEADS):
            k = kv[b * mlen:(b + 1) * mlen, h * LANE:(h + 1) * LANE]
            k = _rms_rows(k, kg)
            kt_ref[0, b, h] = k.T.astype(BF16)
            c0 = MEM_WIDTH + h * LANE
            v_ref[0, b, h] = kv[b * mlen:(b + 1) * mlen, c0:c0 + LANE].astype(BF16)


def _mem_kv(mem, mem_norm_g, w_kv_bf16, k_norm_g):
    batch, mlen, d = mem.shape
    depth = w_kv_bf16.shape[0]
    kern = functools.partial(_memkv_kernel, batch=batch, mlen=mlen)
    return pl.pallas_call(
        kern,
        grid=(depth,),
        in_specs=[
            pl.BlockSpec((batch * mlen, d), lambda i: (0, 0)),
            pl.BlockSpec((1, d), lambda i: (0, 0)),
            pl.BlockSpec((1, d, 2 * MEM_WIDTH), lambda i: (i, 0, 0)),
            pl.BlockSpec((1, 1, MEM_HEAD_DIM), lambda i: (i, 0, 0)),
        ],
        out_specs=[
            pl.BlockSpec((1, batch, MEM_HEADS, MEM_HEAD_DIM, mlen), lambda i: (i, 0, 0, 0, 0)),
            pl.BlockSpec((1, batch, MEM_HEADS, mlen, MEM_HEAD_DIM), lambda i: (i, 0, 0, 0, 0)),
        ],
        out_shape=[
            jax.ShapeDtypeStruct((depth, batch, MEM_HEADS, MEM_HEAD_DIM, mlen), BF16),
            jax.ShapeDtypeStruct((depth, batch, MEM_HEADS, mlen, MEM_HEAD_DIM), BF16),
        ],
        compiler_params=_params(1),
        name="mem_kv",
    )(mem.reshape(batch * mlen, d), mem_norm_g.reshape(1, d), w_kv_bf16,
      k_norm_g.reshape(depth, 1, MEM_HEAD_DIM))


def _in_proj_kernel(x_ref, g_ref, w_ref, hg_ref, *out_refs, segs, norm_groups, chunk):
    xn = _rms_rows(x_ref[...], g_ref[...]).astype(BF16)
    for (c0, width), o_ref in zip(segs, out_refs):
        for cc in range(0, width, chunk):
            cw = min(chunk, width - cc)
            acc = jnp.dot(xn, w_ref[:, c0 + cc:c0 + cc + cw], preferred_element_type=F32)
            for gi in range(cw // LANE):
                col = c0 + cc + gi * LANE
                y = acc[:, gi * LANE:(gi + 1) * LANE]
                if col // LANE in norm_groups:
                    y = _rms_rows(y, hg_ref[:, col:col + LANE])
                if len(o_ref.shape) == 3:
                    o_ref[(cc + gi * LANE) // LANE] = y.astype(o_ref.dtype)
                else:
                    o_ref[:, cc + gi * LANE:cc + (gi + 1) * LANE] = y.astype(o_ref.dtype)


def _in_proj(x2d, norm_g, w_stack, layer, head_gain, segs, out_dtypes, norm_groups, name,
             head_major=()):
    t, d = x2d.shape
    n = w_stack.shape[2]
    tm = min(TOKEN_TILE, t)
    kern = functools.partial(_in_proj_kernel, segs=segs, norm_groups=frozenset(norm_groups),
                             chunk=4 * LANE)
    out_specs, out_shape = [], []
    for idx, ((_, w), dt) in enumerate(zip(segs, out_dtypes)):
        if idx in head_major:
            out_specs.append(pl.BlockSpec((w // LANE, tm, LANE), lambda i: (0, i, 0)))
            out_shape.append(jax.ShapeDtypeStruct((w // LANE, t, LANE), dt))
        else:
            out_specs.append(pl.BlockSpec((tm, w), lambda i: (i, 0)))
            out_shape.append(jax.ShapeDtypeStruct((t, w), dt))
    return pl.pallas_call(
        kern,
        grid=(t // tm,),
        in_specs=[
            pl.BlockSpec((tm, d), lambda i: (i, 0)),
            _resident((1, d)),
            _layer_block(w_stack.shape, layer),
            _resident((1, n)),
        ],
        out_specs=out_specs,
        out_shape=out_shape,
        compiler_params=_params(1),
        name=name,
    )(x2d, norm_g.reshape(1, d), w_stack, head_gain)


def _ssd_kernel(z_ref, xbc_ref, dt_ref, cw_ref, cb_ref, dtb_ref, alog_ref, dexp_ref, ng_ref,
                o_ref, cbuf, state):
    L = SSD_CHUNK
    gw = SSD_HPG * SSD_HEAD_DIM
    c = pl.program_id(1)

    @pl.when(c == 0)
    def _():
        cbuf[0:SUBLANE, :] = jnp.zeros((SUBLANE, SSD_CONV_DIM), F32)
        state[...] = jnp.zeros(state.shape, F32)

    cbuf[SUBLANE:SUBLANE + L, :] = xbc_ref[...].astype(F32)
    acc = jnp.broadcast_to(cb_ref[...], (L, SSD_CONV_DIM))
    for k in range(SSD_CONV):
        acc = acc + cw_ref[k:k + 1, :] * cbuf[pl.ds(SUBLANE - (SSD_CONV - 1) + k, L), :]
    cbuf[0:SUBLANE, :] = cbuf[L:L + SUBLANE, :]
    u = acc * jax.nn.sigmoid(acc)
    xs = u[:, :SSD_INNER]
    bm = u[:, SSD_INNER:SSD_INNER + SSD_BC]
    cm = u[:, SSD_INNER + SSD_BC:]

    lane = lax.broadcasted_iota(jnp.int32, (1, LANE), 1)
    traw = dt_ref[...] + dtb_ref[...]
    dt = jnp.maximum(traw, 0.0) + jnp.log1p(jnp.exp(-jnp.abs(traw)))
    a = jnp.where(lane < SSD_HEADS, -jnp.exp(alog_ref[...]), 0.0)
    da = dt * a
    ri = lax.broadcasted_iota(jnp.int32, (L, L), 0)
    ci = lax.broadcasted_iota(jnp.int32, (L, L), 1)
    causal = ri >= ci
    da3 = jnp.dot(causal.astype(BF16), jnp.concatenate(_split_bf16x3(da), axis=1),
                  preferred_element_type=F32)
    cs = (da3[:, :LANE] + da3[:, LANE:2 * LANE]) + da3[:, 2 * LANE:]
    cs_t = cs.T
    cs_last = cs[L - 1:L, :]
    ecs = jnp.exp(cs)
    dte = jnp.exp(cs_last - cs)
    erow = lax.broadcasted_iota(jnp.int32, (LANE, SSD_INNER), 0)
    ecol = lax.broadcasted_iota(jnp.int32, (LANE, SSD_INNER), 1) // SSD_HEAD_DIM
    expand = (erow == ecol).astype(BF16)
    stacked = jnp.concatenate([dt, ecs, dte], axis=0)
    expanded = jnp.dot(jnp.concatenate(_split_bf16x3(stacked), axis=1),
                       jnp.concatenate([expand] * 3, axis=0), preferred_element_type=F32)
    dt_x, ecs_x, dte_x = expanded[:L], expanded[L:2 * L], expanded[2 * L:]
    xdt = xs * dt_x
    xdt_b = xdt.astype(BF16)
    xdte_b = (xdt * dte_x).astype(BF16)
    cdec_x = ecs_x[L - 1:L, :]
    lo_half = lax.broadcasted_iota(jnp.int32, (L, LANE), 1) < SSD_HEAD_DIM

    z = z_ref[...].astype(F32)
    gate = z * jax.nn.sigmoid(z)
    for g in range(SSD_GROUPS):
        g0 = g * gw
        cg = cm[:, g * SSD_STATE:(g + 1) * SSD_STATE].astype(BF16)
        bg_t = bm[:, g * SSD_STATE:(g + 1) * SSD_STATE].T.astype(BF16)
        cbm = jnp.dot(cg, bg_t, preferred_element_type=F32)
        st = state[g]
        y_g = jnp.dot(cg, st.astype(BF16), preferred_element_type=F32) * ecs_x[:, g0:g0 + gw]
        st_c = jnp.dot(bg_t, xdte_b[:, g0:g0 + gw], preferred_element_type=F32)
        state[g] = st * cdec_x[:, g0:g0 + gw] + st_c
        pairs = []
        for pr in range(SSD_HPG // 2):
            p0 = g0 + pr * LANE
            xp = xdt_b[:, p0:p0 + LANE]
            yd = None
            for half in range(2):
                h = g * SSD_HPG + pr * 2 + half
                seg = jnp.broadcast_to(cs[:, h:h + 1], (L, L)) - cs_t[h:h + 1, :]
                dec = jnp.where(causal, jnp.exp(seg), 0.0)
                wm = (cbm * dec).astype(BF16)
                keep = lo_half if half == 0 else jnp.logical_not(lo_half)
                part = jnp.dot(wm, jnp.where(keep, xp, jnp.zeros_like(xp)),
                               preferred_element_type=F32)
                yd = part if yd is None else yd + part
            pairs.append(yd)
        y_g = y_g + jnp.concatenate(pairs, axis=1) + dexp_ref[:, g0:g0 + gw] * xs[:, g0:g0 + gw]
        y_g = y_g * gate[:, g0:g0 + gw]
        o_ref[:, g0:g0 + gw] = _rms_rows(y_g, ng_ref[:, g0:g0 + gw]).astype(o_ref.dtype)


def _ssd(z, xbc, dt, conv_w, conv_b, dt_bias, a_log, d_skip, norm_g, batch, seq):
    t = batch * seq
    nc = seq // SSD_CHUNK
    L = SSD_CHUNK
    pad = LANE - SSD_HEADS
    row = lambda b, c: (b * nc + c, 0)
    return pl.pallas_call(
        _ssd_kernel,
        grid=(batch, nc),
        in_specs=[
            pl.BlockSpec((L, SSD_INNER), row),
            pl.BlockSpec((L, SSD_CONV_DIM), row),
            pl.BlockSpec((L, LANE), row),
            _resident((SSD_CONV, SSD_CONV_DIM)),
            _resident((1, SSD_CONV_DIM)),
            _resident((1, LANE)),
            _resident((1, LANE)),
            _resident((1, SSD_INNER)),
            _resident((1, SSD_INNER)),
        ],
        out_specs=pl.BlockSpec((L, SSD_INNER), row),
        out_shape=jax.ShapeDtypeStruct((t, SSD_INNER), BF16),
        scratch_shapes=[
            pltpu.VMEM((L + 2 * SUBLANE, SSD_CONV_DIM), F32),
            pltpu.VMEM((SSD_GROUPS, SSD_STATE, SSD_HPG * SSD_HEAD_DIM), F32),
        ],
        compiler_params=_params(2),
        name="ssd_mix",
    )(z, xbc, dt, conv_w, conv_b.reshape(1, -1),
      jnp.pad(dt_bias, (0, pad)).reshape(1, LANE), jnp.pad(a_log, (0, pad)).reshape(1, LANE),
      jnp.repeat(d_skip, SSD_HEAD_DIM).reshape(1, SSD_INNER), norm_g.reshape(1, SSD_INNER))


def _sb_kernel(drow_ref, dkey_ref, brow_ref, bkey_ref, q_ref, k_ref, v_ref, o_ref,
               acc_ref, carry_ref, n_ref, lm_ref, w_ref, *, kb, chunk, n_diag, n_bulk):
    acc_ref[...] = jnp.zeros(acc_ref.shape, F32)
    carry_ref[...] = jnp.zeros(carry_ref.shape, F32)
    tj = lax.broadcasted_iota(jnp.int32, (kb, kb), 0)
    ts = lax.broadcasted_iota(jnp.int32, (kb, kb), 1)
    suffix = (tj >= ts).astype(BF16)
    sign = jnp.uint32(0x80000000)
    sign16 = jnp.uint16(0x8000)

    def run(n_pairs, m, masked, row_ref, key_ref):
        if masked:
            col_minus_row = (lax.broadcasted_iota(jnp.int32, (chunk, kb), 1)
                             - lax.broadcasted_iota(jnp.int32, (chunk, kb), 0))

        def offsets(f):
            return pl.multiple_of(row_ref[f], kb), pl.multiple_of(key_ref[f], kb)

        def scores_and_log_terms(f, s):
            r, k = offsets(f)
            n_all = lax.dot_general(
                q_ref[pl.ds(r, m), :], k_ref[pl.ds(k, kb), :], (((1,), (1,)), ((), ())),
                preferred_element_type=F32)
            n_ref[s, 0:m, :] = n_all
            for c0 in range(0, m, chunk):
                n = n_all[c0:c0 + chunk, :]
                if masked:
                    neg_abs = pltpu.bitcast(pltpu.bitcast(n, jnp.uint32) | sign, F32)
                    soft = jnp.log(1.0 + jnp.exp2(neg_abs)) * LOG2E
                    l1m = jnp.minimum(n, 0.0) - soft
                    l1m = jnp.where(col_minus_row < r - k + c0, l1m, 0.0)
                    lm_ref[s, c0:c0 + chunk, :] = l1m.astype(BF16)
                else:
                    nb = n.astype(BF16)
                    neg_abs = pltpu.bitcast(pltpu.bitcast(nb, jnp.uint16) | sign16, BF16)
                    soft = jnp.log(1.0 + jnp.exp2(neg_abs)) * LOG2E
                    lm_ref[s, c0:c0 + chunk, :] = jnp.minimum(nb, 0.0) - soft

        def suffix_sums_and_weights(f, s):
            r, k = offsets(f)
            incl_all = jnp.dot(lm_ref[s, 0:m, :], suffix, preferred_element_type=F32)
            for c0 in range(0, m, chunk):
                incl = incl_all[c0:c0 + chunk, :]
                carry = carry_ref[pl.ds(r + c0, chunk), :]
                e = (incl - n_ref[s, c0:c0 + chunk, :]) + jnp.concatenate(
                    [carry] * (kb // LANE), axis=1)
                w = jnp.exp2(e)
                if masked:
                    w = jnp.where(col_minus_row < r - k + c0, w, 0.0)
                w_ref[s, c0:c0 + chunk, :] = w.astype(BF16)
                carry_ref[pl.ds(r + c0, chunk), :] = carry + incl[:, 0:1]

        def weighted_values(f, s):
            r, k = offsets(f)
            acc_ref[pl.ds(r, m), :] += jnp.dot(w_ref[s, 0:m, :], v_ref[pl.ds(k, kb), :],
                                               preferred_element_type=F32)

        stages = (scores_and_log_terms, suffix_sums_and_weights, weighted_values)
        depth = len(stages)

        def step(sigma, phase, first, last):
            for d in reversed(range(depth)):
                if first <= d <= last:
                    stages[d](sigma - d, (phase - d) % SB_SLOTS)

        def static_step(sigma):
            step(sigma, sigma % SB_SLOTS, max(0, sigma - n_pairs + 1), min(depth - 1, sigma))

        n_loop = max(0, (n_pairs - SB_SLOTS) // SB_SLOTS)
        loop_start = SB_SLOTS
        assert loop_start >= depth - 1
        for sigma in range(min(loop_start, n_pairs + depth - 1)):
            static_step(sigma)
        if n_loop > 0:
            def body(i, c):
                for phase in range(SB_SLOTS):
                    step(loop_start + i * SB_SLOTS + phase, phase, 0, depth - 1)
                return c
            lax.fori_loop(0, n_loop, body, 0)
        for sigma in range(loop_start + n_loop * SB_SLOTS, n_pairs + depth - 1):
            static_step(sigma)

    run(n_diag, kb, True, drow_ref, dkey_ref)
    run(n_bulk, 2 * kb, False, brow_ref, bkey_ref)
    o_ref[...] = acc_ref[...].astype(o_ref.dtype)


def _sb_pair_tables(seq, kb):
    diag, bulk = [], []
    for t in range(seq // (2 * kb)):
        lo, hi = 2 * t * kb, (2 * t + 1) * kb
        diag += [(lo, lo), (hi, hi), (hi, lo)]
        bulk += [(lo, j * kb) for j in reversed(range(2 * t))]
    as_i32 = lambda xs: jnp.asarray(xs, jnp.int32)
    return (as_i32([p[0] for p in diag]), as_i32([p[1] for p in diag]),
            as_i32([p[0] for p in bulk] or [0]), as_i32([p[1] for p in bulk] or [0]),
            len(diag), len(bulk))


def _sb_attn(qkv, batch, seq):
    t = batch * seq
    kb = SB_KEY_BLOCK
    assert seq % (4 * kb) == 0
    dh = SB_HEAD_DIM
    drow, dkey, brow, bkey, n_diag, n_bulk = _sb_pair_tables(seq, kb)
    kern = functools.partial(_sb_kernel, kb=kb, chunk=SB_ROW_CHUNK, n_diag=n_diag, n_bulk=n_bulk)
    ring = (SB_SLOTS, 2 * kb, kb)
    grid_spec = pltpu.PrefetchScalarGridSpec(
        num_scalar_prefetch=4,
        grid=(batch, SB_HEADS),
        in_specs=[
            pl.BlockSpec((None, seq, dh), lambda b, h, *_: (h, b, 0)),
            pl.BlockSpec((None, seq, dh), lambda b, h, *_: (SB_HEADS + h, b, 0)),
            pl.BlockSpec((None, seq, dh), lambda b, h, *_: (2 * SB_HEADS + h, b, 0)),
        ],
        out_specs=pl.BlockSpec((None, seq, dh), lambda b, h, *_: (h, b, 0)),
        scratch_shapes=[
            pltpu.VMEM((seq, dh), F32),
            pltpu.VMEM((seq, LANE), F32),
            pltpu.VMEM(ring, F32),
            pltpu.VMEM(ring, BF16),
            pltpu.VMEM(ring, BF16),
        ],
    )
    return pl.pallas_call(
        kern,
        grid_spec=grid_spec,
        out_shape=jax.ShapeDtypeStruct((SB_HEADS, t, dh), BF16),
        compiler_params=_params(2),
        name="sb_attn",
    )(drow, dkey, brow, bkey, qkv, qkv, qkv)


def _out_kernel(x_ref, y_ref, qm_ref, kt_ref, v_ref, w_ref, o_ref):
    qm = qm_ref[...]
    heads = []
    for h in range(MEM_HEADS):
        s = jnp.dot(qm[:, h * LANE:(h + 1) * LANE], kt_ref[0, h], preferred_element_type=F32)
        m = jnp.max(s, axis=-1, keepdims=True)
        p = jnp.exp2(s - m)
        l = jnp.sum(p, axis=-1, keepdims=True)
        o = jnp.dot(p.astype(BF16), v_ref[0, h], preferred_element_type=F32) / l
        heads.append(o.astype(BF16))
    om = jnp.concatenate(heads, axis=1)
    if len(y_ref.shape) == 3:
        y = jnp.concatenate([y_ref[h] for h in range(y_ref.shape[0])], axis=1)
    else:
        y = y_ref[...]
    ny = y.shape[1]
    acc = x_ref[...] + jnp.dot(y, w_ref[:ny, :], preferred_element_type=F32)
    o_ref[...] = acc + jnp.dot(om, w_ref[ny:, :], preferred_element_type=F32)


def _out_proj(x2d, y, qm_arr, qm_col_block, mem_kt, mem_v, mem_layer, w_out_stack, w_layer, seq):
    t, d = x2d.shape
    tm = min(TOKEN_TILE, seq)
    per_batch = seq // tm
    mlen = mem_v.shape[3]
    if y.ndim == 3:
        y_spec = pl.BlockSpec((y.shape[0], tm, y.shape[2]), lambda i: (0, i, 0))
    else:
        y_spec = pl.BlockSpec((tm, y.shape[1]), lambda i: (i, 0))
    return pl.pallas_call(
        _out_kernel,
        grid=(t // tm,),
        in_specs=[
            pl.BlockSpec((tm, d), lambda i: (i, 0)),
            y_spec,
            pl.BlockSpec((tm, MEM_WIDTH), lambda i: (i, qm_col_block)),
            pl.BlockSpec((None, 1, MEM_HEADS, MEM_HEAD_DIM, mlen),
                         lambda i: (mem_layer, i // per_batch, 0, 0, 0)),
            pl.BlockSpec((None, 1, MEM_HEADS, mlen, MEM_HEAD_DIM),
                         lambda i: (mem_layer, i // per_batch, 0, 0, 0)),
            _layer_block(w_out_stack.shape, w_layer),
        ],
        out_specs=pl.BlockSpec((tm, d), lambda i: (i, 0)),
        out_shape=jax.ShapeDtypeStruct((t, d), F32),
        compiler_params=_params(1),
        name="out_proj",
    )(x2d, y, qm_arr, mem_kt, mem_v, w_out_stack)


def _ffn_kernel(x_ref, g_ref, wgu_ref, wd_ref, o_ref, *, hidden, chunk):
    x = x_ref[...]
    xn = _rms_rows(x, g_ref[...]).astype(BF16)
    acc = x
    for c0 in range(0, hidden, chunk):
        gt = jnp.dot(xn, wgu_ref[:, c0:c0 + chunk], preferred_element_type=F32)
        up = jnp.dot(xn, wgu_ref[:, hidden + c0:hidden + c0 + chunk], preferred_element_type=F32)
        hh = (gt * jax.nn.sigmoid(gt) * up).astype(BF16)
        acc = acc + jnp.dot(hh, wd_ref[c0:c0 + chunk, :], preferred_element_type=F32)
    o_ref[...] = acc


def _ffn(x2d, norm_g, wgu_stack, wd_stack, layer):
    t, d = x2d.shape
    hidden = wd_stack.shape[1]
    tm = min(TOKEN_TILE, t)
    kern = functools.partial(_ffn_kernel, hidden=hidden, chunk=FFN_HIDDEN_CHUNK)
    return pl.pallas_call(
        kern,
        grid=(t // tm,),
        in_specs=[
            pl.BlockSpec((tm, d), lambda i: (i, 0)),
            _resident((1, d)),
            _layer_block(wgu_stack.shape, layer),
            _layer_block(wd_stack.shape, layer),
        ],
        out_specs=pl.BlockSpec((tm, d), lambda i: (i, 0)),
        out_shape=jax.ShapeDtypeStruct((t, d), F32),
        compiler_params=_params(1),
        name="ffn",
    )(x2d, norm_g.reshape(1, d), wgu_stack, wd_stack)


SSD_XBC_END = SSD_INNER + SSD_CONV_DIM


def _ssd_w_in_regrouped(w_in):
    dt_end = SSD_XBC_END + SSD_HEADS
    pad = jnp.zeros(w_in.shape[:2] + (LANE - SSD_HEADS,), w_in.dtype)
    return jnp.concatenate([w_in[..., :SSD_XBC_END], w_in[..., dt_end:],
                            w_in[..., SSD_XBC_END:dt_end], pad], axis=-1).astype(BF16)


def _ssd_layer(x2d, mix_g, w_in_stack, j, conv_w, conv_b, dt_bias, a_log, d_skip, norm_g,
               w_out_stack, mq_g, mem_kt, mem_v, mem_layer, batch, seq):
    qm0 = SSD_XBC_END
    segs = ((0, SSD_INNER), (SSD_INNER, SSD_CONV_DIM), (qm0, MEM_WIDTH), (qm0 + MEM_WIDTH, LANE))
    gain = jnp.ones((w_in_stack.shape[2],), F32).at[qm0:qm0 + MEM_WIDTH].set(
        jnp.tile(mq_g * (MEM_HEAD_DIM ** -0.5 * LOG2E), MEM_HEADS)).reshape(1, -1)
    groups = range(qm0 // LANE, (qm0 + MEM_WIDTH) // LANE)
    z, xbc, qm, dt = _in_proj(x2d, mix_g, w_in_stack, j, gain, segs, (BF16, BF16, BF16, F32),
                              groups, "in_proj_ssd")
    y = _ssd(z, xbc, dt, conv_w, conv_b, dt_bias, a_log, d_skip, norm_g, batch, seq)
    return _out_proj(x2d, y, qm, 0, mem_kt, mem_v, mem_layer, w_out_stack, j, seq)


def _sb_layer(x2d, mix_g, w_in_stack, j, q_g, k_g, w_out_stack, mq_g, mem_kt, mem_v, mem_layer,
              batch, seq):
    n = w_in_stack.shape[2]
    qm0 = 3 * SB_WIDTH
    gain = jnp.concatenate([
        jnp.tile(q_g * (-(SB_HEAD_DIM ** -0.5) * LOG2E), SB_HEADS),
        jnp.tile(k_g, SB_HEADS),
        jnp.ones((SB_WIDTH,), F32),
        jnp.tile(mq_g * (MEM_HEAD_DIM ** -0.5 * LOG2E), MEM_HEADS)]).reshape(1, n)
    groups = list(range(0, 2 * SB_HEADS)) + list(range(qm0 // LANE, n // LANE))
    qkv, qm = _in_proj(x2d, mix_g, w_in_stack, j, gain, ((0, qm0), (qm0, MEM_WIDTH)),
                       (BF16, BF16), groups, "in_proj_sb", head_major=(0,))
    o = _sb_attn(qkv, batch, seq)
    return _out_proj(x2d, o, qm, 0, mem_kt, mem_v, mem_layer, w_out_stack, j, seq)


def kernel(x, mem, mix_norm_g, ffn_norm_g, mem_norm_g, mem_w_kv, mem_q_norm_g, mem_k_norm_g, ssd_w_in, ssd_conv_w, ssd_conv_b, ssd_dt_bias, ssd_a_log, ssd_d, ssd_norm_g, ssd_w_out, sb_w_in, sb_q_norm_g, sb_k_norm_g, sb_w_out, ffn_w_gate_up, ffn_w_down):
    batch, seq, d = x.shape
    x2d = x.reshape(batch * seq, d)
    mem_kt, mem_v = _mem_kv(mem, mem_norm_g, mem_w_kv.astype(BF16), mem_k_norm_g)
    ssd_w_in_b = _ssd_w_in_regrouped(ssd_w_in)
    ssd_w_out_b = ssd_w_out.astype(BF16)
    sb_w_in_b = sb_w_in.astype(BF16)
    sb_w_out_b = sb_w_out.astype(BF16)
    wgu_b = ffn_w_gate_up.astype(BF16)
    wd_b = ffn_w_down.astype(BF16)
    for i in range(DEPTH):
        j = i // 2
        if i % 2 == 0:
            x2d = _ssd_layer(x2d, mix_norm_g[i], ssd_w_in_b, j, ssd_conv_w[j], ssd_conv_b[j],
                             ssd_dt_bias[j], ssd_a_log[j], ssd_d[j], ssd_norm_g[j], ssd_w_out_b,
                             mem_q_norm_g[i], mem_kt, mem_v, i, batch, seq)
        else:
            x2d = _sb_layer(x2d, mix_norm_g[i], sb_w_in_b, j, sb_q_norm_g[j], sb_k_norm_g[j],
                            sb_w_out_b, mem_q_norm_g[i], mem_kt, mem_v, i, batch, seq)
        x2d = _ffn(x2d, ffn_norm_g[i], wgu_b, wd_b, i)
    return x2d.reshape(batch, seq, d)
```
